```python
import jax, jax.numpy as jnp
from jax import lax
import numpy as np

D_MODEL = 1024
BATCH = 4
SEQ = 4096
DEPTH = 1

N_META = 16
EPS = 1e-6
D_FF = 2816
CHUNK = 64
A_DK = 128
A_DV = 128
A_HEADS = D_MODEL // A_DV
A_CONV = 4
A_WK = A_HEADS * A_DK
A_WV = A_HEADS * A_DV
B_N = 64
B_HEADS = D_MODEL // B_N
B_W = B_HEADS * B_N
W_LORA = 64
AA_LORA = 64
G_LORA = 160
B_GN_EPS = B_N * 1e-5
B_COLS = 3 * B_W + W_LORA + AA_LORA + G_LORA
IN_SIZES = (A_WK, A_WK, A_WV, A_WV, A_HEADS, A_HEADS, B_COLS, D_MODEL, D_MODEL)
IN_TOTAL = sum(IN_SIZES)

kernel_name = "meta_macaron_deltanet_rwkv7_hybrid"


def _offsets(sizes):
    out, acc = [], 0
    for s in sizes[:-1]:
        acc += s
        out.append(acc)
    return out


def _rmsnorm(x, gain):
    xf = x.astype(jnp.float32)
    y = xf * lax.rsqrt(jnp.mean(xf * xf, axis=-1, keepdims=True) + EPS)
    return (y * gain.astype(jnp.float32)).astype(x.dtype)


def _l2norm(x):
    xf = x.astype(jnp.float32)
    return xf * lax.rsqrt(jnp.sum(xf * xf, axis=-1, keepdims=True) + 1e-6)


def _swiglu(x, w_gu, w_down):
    gate, up = jnp.split(x @ w_gu, 2, axis=-1)
    return (jax.nn.silu(gate) * up) @ w_down


def _causal_dwconv(x, w):
    k = w.shape[0]
    return lax.conv_general_dilated(x, w[:, None, :].astype(x.dtype), window_strides=(1,),
                                    padding=[(k - 1, 0)], dimension_numbers=('NWC', 'WIO', 'NWC'),
                                    feature_group_count=x.shape[-1])


def _gated_delta_chunked(q, k, v, beta, g):
    b, h, t, dk = q.shape
    dv = v.shape[-1]
    n = t // CHUNK

    def ch(z):
        return z.reshape((b, h, n, CHUNK) + z.shape[3:])

    q, k, v, beta, g = ch(q), ch(k), ch(v), ch(beta), ch(g)
    g = jnp.cumsum(g, axis=-1)
    kb = k * beta[..., None]
    vb = v * beta[..., None]
    idx = jnp.arange(CHUNK)
    incl = idx[:, None] >= idx[None, :]
    strict = idx[:, None] > idx[None, :]
    diff = g[..., :, None] - g[..., None, :]
    decay = jnp.where(incl, jnp.exp(jnp.where(incl, diff, 0.0)), 0.0)
    m = jnp.where(strict, jnp.einsum('bhncd,bhnsd->bhncs', kb, k) * decay, 0.0)
    eye = jnp.eye(CHUNK, dtype=m.dtype)
    tinv = lax.linalg.triangular_solve(m + eye, jnp.broadcast_to(eye, m.shape), left_side=True,
                                       lower=True, unit_diagonal=True)
    u = jnp.einsum('bhncs,bhnsd->bhncd', tinv, vb)
    wk = jnp.einsum('bhncs,bhnsd->bhncd', tinv, kb * jnp.exp(g)[..., None])
    attn = jnp.einsum('bhncd,bhnsd->bhncs', q, k) * decay
    qg = q * jnp.exp(g)[..., None]
    g_last = g[..., -1]
    k_tail = k * jnp.exp(g_last[..., None] - g)[..., None]

    def step(state, inp):
        u_i, w_i, attn_i, qg_i, kt_i, gl_i = inp
        v_new = u_i - jnp.einsum('bhcd,bhde->bhce', w_i, state)
        o = jnp.einsum('bhcd,bhde->bhce', qg_i, state) + jnp.einsum('bhcs,bhse->bhce', attn_i, v_new)
        state = state * jnp.exp(gl_i)[..., None, None] + jnp.einsum('bhcd,bhce->bhde', kt_i, v_new)
        return state, o

    xs = (jnp.moveaxis(u, 2, 0), jnp.moveaxis(wk, 2, 0), jnp.moveaxis(attn, 2, 0),
          jnp.moveaxis(qg, 2, 0), jnp.moveaxis(k_tail, 2, 0), jnp.moveaxis(g_last, 2, 0))
    s0 = jnp.zeros((b, h, dk, dv), jnp.float32)
    _, o = lax.scan(step, s0, xs)
    return jnp.moveaxis(o, 0, 2).reshape(b, h, t, dv)


def _deltanet_branch(q, k, v, z, beta_pre, alpha_pre, conv_w, log_rate, dt_bias, out_gain):
    bsz, t, _ = q.shape
    qkv = jax.nn.silu(_causal_dwconv(jnp.concatenate([q, k, v], axis=-1), conv_w))
    q, k, v = jnp.split(qkv, [A_WK, 2 * A_WK], axis=-1)
    q = _l2norm(q.reshape(bsz, t, A_HEADS, A_DK)) * (A_DK ** -0.5)
    k = _l2norm(k.reshape(bsz, t, A_HEADS, A_DK))
    v = v.reshape(bsz, t, A_HEADS, A_DV).astype(jnp.float32)
    beta = jax.nn.sigmoid(beta_pre.astype(jnp.float32))
    g = -jnp.exp(log_rate.astype(jnp.float32)) * jax.nn.softplus(
        alpha_pre.astype(jnp.float32) + dt_bias.astype(jnp.float32))
    pad = CHUNK - N_META

    def prep(a):
        a = jnp.pad(a, ((0, 0), (pad, 0)) + ((0, 0),) * (a.ndim - 2))
        return jnp.moveaxis(a, 2, 1)

    o = _gated_delta_chunked(prep(q), prep(k), prep(v), prep(beta), prep(g))
    o = jnp.moveaxis(o, 1, 2)[:, pad:]
    o = o * lax.rsqrt(jnp.mean(o * o, axis=-1, keepdims=True) + EPS) * out_gain.astype(jnp.float32)
    o = o.reshape(bsz, t, A_WV) * jax.nn.silu(z.astype(jnp.float32))
    return o.astype(z.dtype)


def _rwkv7_scan(r, w, k, v, a_vec, b_vec):
    def step(state, inp):
        r_t, w_t, k_t, v_t, a_t, b_t = inp
        sa = jnp.einsum('bhvk,bhk->bhv', state, a_t)
        state = state * w_t[:, :, None, :] + sa[..., None] * b_t[:, :, None, :] \
            + v_t[..., None] * k_t[:, :, None, :]
        return state, jnp.einsum('bhvk,bhk->bhv', state, r_t)

    bsz, t, h, n = r.shape
    xs = tuple(jnp.moveaxis(a, 1, 0) for a in (r, w, k, v, a_vec, b_vec))
    s0 = jnp.zeros((bsz, h, n, n), jnp.float32)
    _, y = lax.scan(step, s0, xs)
    return jnp.moveaxis(y, 0, 1)


def _rwkv7_branch(zb, mu, w0, w_up, a0, a_up, g_up, k_k, k_a, r_k, ln_gain, ln_bias):
    bsz, t, _ = zb.shape
    f32 = jnp.float32
    zf = zb.astype(f32)
    prev = jnp.pad(zf, ((0, 0), (1, 0), (0, 0)))[:, :-1]
    zf = zf + (prev - zf) * mu.astype(f32)
    r, k, v, wd, ad, gd = jnp.split(
        zf, [B_W, 2 * B_W, 3 * B_W, 3 * B_W + W_LORA, 3 * B_W + W_LORA + AA_LORA], axis=-1)
    w_log = -jax.nn.softplus(-(w0.astype(f32) + jnp.tanh(wd) @ w_up.astype(f32))) - 0.5
    decay = jnp.exp(-jnp.exp(w_log))
    a = jax.nn.sigmoid(a0.astype(f32) + ad @ a_up.astype(f32))
    gate = jax.nn.sigmoid(gd) @ g_up.astype(f32)
    hs = (bsz, t, B_HEADS, B_N)
    kk = _l2norm((k * k_k.astype(f32)).reshape(hs))
    k = k * (1.0 + (a - 1.0) * k_a.astype(f32))
    r, k, v, decay, a = r.reshape(hs), k.reshape(hs), v.reshape(hs), decay.reshape(hs), a.reshape(hs)
    y = _rwkv7_scan(r, decay, k, v, -kk, kk * a)
    mean = jnp.mean(y, axis=-1, keepdims=True)
    var = jnp.mean(jnp.square(y - mean), axis=-1, keepdims=True)
    y = (y - mean) * lax.rsqrt(var + B_GN_EPS) * ln_gain.astype(f32).reshape(B_HEADS, B_N) \
        + ln_bias.astype(f32).reshape(B_HEADS, B_N)
    y = y + jnp.sum(r * k * r_k.astype(f32), axis=-1, keepdims=True) * v
    return (y.reshape(bsz, t, B_W) * gate).astype(zb.dtype)


def setup_inputs(seed: int = 0) -> dict:
    key = jax.random.key(seed)
    ks = jax.random.split(key, 32)
    f32 = jnp.float32

    def nrm(k, shape, scale):
        return jax.random.normal(k, shape, f32) * scale

    def gain(k, shape):
        return 1.0 + 0.02 * jax.random.normal(k, shape, f32)

    dt = jnp.exp(jax.random.uniform(ks[8], (DEPTH, A_HEADS), f32, np.log(1e-3), np.log(1e-1)))
    return {
        "x": nrm(ks[0], (BATCH, SEQ, D_MODEL), 1.0),
        "meta_tokens": nrm(ks[1], (N_META, D_MODEL), 1.0),
        "ffn1_norm": gain(ks[2], (DEPTH, D_MODEL)),
        "ffn1_w_gu": nrm(ks[3], (DEPTH, D_MODEL, 2 * D_FF), D_MODEL ** -0.5),
        "ffn1_w_down": nrm(ks[4], (DEPTH, D_FF, D_MODEL), D_FF ** -0.5),
        "mix_norm": gain(ks[5], (DEPTH, D_MODEL)),
        "w_in": nrm(ks[6], (DEPTH, D_MODEL, IN_TOTAL), D_MODEL ** -0.5),
        "a_conv_w": nrm(ks[7], (DEPTH, A_CONV, 2 * A_WK + A_WV), A_CONV ** -0.5),
        "a_log_rate": jnp.log(jax.random.uniform(ks[9], (DEPTH, A_HEADS), f32, 1.0, 16.0)),
        "a_dt_bias": dt + jnp.log(-jnp.expm1(-dt)),
        "a_out_norm": gain(ks[10], (DEPTH, A_DV)),
        "b_shift_mu": jax.random.uniform(ks[11], (DEPTH, B_COLS), f32, 0.0, 1.0),
        "b_w0": jax.random.uniform(ks[12], (DEPTH, B_W), f32, -6.5, -1.5),
        "b_w_up": nrm(ks[13], (DEPTH, W_LORA, B_W), 0.5 * W_LORA ** -0.5),
        "b_a0": nrm(ks[14], (DEPTH, B_W), 0.1),
        "b_a_up": nrm(ks[15], (DEPTH, AA_LORA, B_W), AA_LORA ** -0.5),
        "b_g_up": nrm(ks[16], (DEPTH, G_LORA, B_W), G_LORA ** -0.5),
        "b_k_k": 0.85 + 0.05 * jax.random.normal(ks[17], (DEPTH, B_W), f32),
        "b_k_a": 1.0 + 0.05 * jax.random.normal(ks[18], (DEPTH, B_W), f32),
        "b_r_k": nrm(ks[19], (DEPTH, B_HEADS, B_N), 0.1),
        "b_ln_gain": gain(ks[20], (DEPTH, B_W)),
        "b_ln_bias": nrm(ks[21], (DEPTH, B_W), 0.02),
        "w_out": nrm(ks[22], (DEPTH, D_MODEL, D_MODEL), D_MODEL ** -0.5),
        "ffn2_norm": gain(ks[23], (DEPTH, D_MODEL)),
        "ffn2_w_gu": nrm(ks[24], (DEPTH, D_MODEL, 2 * D_FF), D_MODEL ** -0.5),
        "ffn2_w_down": nrm(ks[25], (DEPTH, D_FF, D_MODEL), D_FF ** -0.5),
        "final_norm": gain(ks[26], (D_MODEL,)),
    }


def reference(x, meta_tokens, ffn1_norm, ffn1_w_gu, ffn1_w_down, mix_norm, w_in, a_conv_w,
              a_log_rate, a_dt_bias, a_out_norm, b_shift_mu, b_w0, b_w_up, b_a0, b_a_up, b_g_up,
              b_k_k, b_k_a, b_r_k, b_ln_gain, b_ln_bias, w_out, ffn2_norm, ffn2_w_gu, ffn2_w_down,
              final_norm):
    bsz = x.shape[0]
    meta = jnp.broadcast_to(meta_tokens[None].astype(x.dtype), (bsz, N_META, D_MODEL))
    h = jnp.concatenate([meta, x], axis=1)
    for l in range(DEPTH):
        h = h + 0.5 * _swiglu(_rmsnorm(h, ffn1_norm[l]), ffn1_w_gu[l], ffn1_w_down[l])
        u = _rmsnorm(h, mix_norm[l])
        aq, ak, av, az, abeta, aalpha, bcols, ga, gb = jnp.split(u @ w_in[l], _offsets(IN_SIZES), axis=-1)
        o_a = _deltanet_branch(aq, ak, av, az, abeta, aalpha, a_conv_w[l], a_log_rate[l],
                               a_dt_bias[l], a_out_norm[l])
        o_b = _rwkv7_branch(bcols, b_shift_mu[l], b_w0[l], b_w_up[l], b_a0[l], b_a_up[l], b_g_up[l],
                            b_k_k[l], b_k_a[l], b_r_k[l], b_ln_gain[l], b_ln_bias[l])
        merged = jax.nn.sigmoid(ga) * o_a + jax.nn.sigmoid(gb) * o_b
        h = h + merged @ w_out[l]
        h = h + 0.5 * _swiglu(_rmsnorm(h, ffn2_norm[l]), ffn2_w_gu[l], ffn2_w_down[l])
    return _rmsnorm(h, final_norm)[:, N_META:]
```

```python
import functools

import jax
import jax.numpy as jnp
from jax import lax
from jax.experimental import pallas as pl
from jax.experimental.pallas import tpu as pltpu

F32 = jnp.float32
BF16 = jnp.bfloat16
HI = lax.Precision.HIGHEST

D_MODEL = 1024
N_META = 16
EPS = 1e-6
D_FF = 2816
CHUNK = 64
A_DK = 128
A_HEADS = 8
A_CONV = 4
B_N = 64
B_HEADS = 16
W_LORA = 64
AA_LORA = 64
G_LORA = 160
B_GN_EPS = B_N * 1e-5

PAD_ROWS = CHUNK - N_META
FF_CHUNK = 256
TAIL = 8

COL_A = 0
COL_G = 4096
COL_B = 6144
COL_S = 9216
S_BA = 0
S_WA = 128
S_GD = 256
IN_COLS = 9728

VMEM_LIMIT = 56 * 1024 * 1024


def _dot(a, b, prec=HI):
    return jnp.dot(a, b, precision=prec, preferred_element_type=F32)


def _dot_nt(a, b, prec=HI):
    return lax.dot_general(a, b, (((1,), (1,)), ((), ())), precision=prec,
                           preferred_element_type=F32)


def _dot_tn(a, b, prec=HI):
    return lax.dot_general(a, b, (((0,), (0,)), ((), ())), precision=prec,
                           preferred_element_type=F32)


def _sigmoid(x):
    return 1.0 / (1.0 + jnp.exp(-x))


def _silu(x):
    return x * _sigmoid(x)


def _softplus(x):
    return jnp.maximum(x, 0.0) + jnp.log(1.0 + jnp.exp(-jnp.abs(x)))


def _rms(x, gain):
    return x * lax.rsqrt(jnp.mean(x * x, axis=-1, keepdims=True) + EPS) * gain


def _tri_masks():
    row = lax.broadcasted_iota(jnp.int32, (CHUNK, CHUNK), 0)
    col = lax.broadcasted_iota(jnp.int32, (CHUNK, CHUNK), 1)
    blocks = [(row >> s) == (col >> s) for s in (3, 4, 5)]
    return row, col, blocks


def _tri_inv(m, row, col, blocks):
    eye = (row == col).astype(F32)
    n1 = jnp.where(blocks[0], m, 0.0)
    n2 = _dot(n1, n1)
    n4 = _dot(n2, n2)
    d = _dot(_dot(eye - n1, eye + n2), eye + n4)
    inner = blocks[0]
    for outer in (blocks[1], blocks[2], None):
        off = jnp.logical_not(inner) if outer is None else jnp.logical_and(outer, jnp.logical_not(inner))
        e = jnp.where(off, m, 0.0)
        d = d - _dot(_dot(d, e), d)
        inner = outer
    return d


def _swiglu(xn, wgu_ref, wd_ref, act_ref):
    for c in range(D_FF // FF_CHUNK):
        lo = c * FF_CHUNK
        g = jnp.dot(xn, wgu_ref[:, lo:lo + FF_CHUNK], preferred_element_type=F32)
        u = jnp.dot(xn, wgu_ref[:, D_FF + lo:D_FF + lo + FF_CHUNK], preferred_element_type=F32)
        act_ref[:, lo:lo + FF_CHUNK] = (_silu(g) * u).astype(BF16)
    return jnp.dot(act_ref[...], wd_ref[...], preferred_element_type=F32)


def _ffn1_kernel(h_ref, gain_ref, wgu_ref, wd_ref, o_ref, act_ref):
    h = h_ref[...]
    xn = _rms(h, gain_ref[...]).astype(BF16)
    o_ref[...] = h + 0.5 * _swiglu(xn, wgu_ref, wd_ref, act_ref)


def _ffn1(h, gain, wgu, wd, tm):
    n = h.shape[0]
    const = dict(pipeline_mode=pl.Buffered(1))
    return pl.pallas_call(
        _ffn1_kernel,
        grid=(n // tm,),
        in_specs=[
            pl.BlockSpec((tm, D_MODEL), lambda i: (i, 0)),
            pl.BlockSpec((1, D_MODEL), lambda i: (0, 0)),
            pl.BlockSpec((D_MODEL, 2 * D_FF), lambda i: (0, 0), **const),
            pl.BlockSpec((D_FF, D_MODEL), lambda i: (0, 0), **const),
        ],
        out_specs=pl.BlockSpec((tm, D_MODEL), lambda i: (i, 0)),
        out_shape=jax.ShapeDtypeStruct((n, D_MODEL), F32),
        scratch_shapes=[pltpu.VMEM((tm, D_FF), BF16)],
        compiler_params=pltpu.CompilerParams(
            dimension_semantics=("parallel",), vmem_limit_bytes=VMEM_LIMIT),
        name="ffn1",
    )(h, gain, wgu, wd)


def _inproj_kernel(h_ref, gain_ref, w_ref, o_ref, u_ref):
    @pl.when(pl.program_id(1) == 0)
    def _():
        u_ref[...] = _rms(h_ref[...], gain_ref[...]).astype(BF16)

    o_ref[...] = jnp.dot(u_ref[...], w_ref[...], preferred_element_type=F32)


def _inproj(h, gain, w, tm, tn):
    n = h.shape[0]
    return pl.pallas_call(
        _inproj_kernel,
        grid=(n // tm, IN_COLS // tn),
        in_specs=[
            pl.BlockSpec((tm, D_MODEL), lambda i, j: (i, 0)),
            pl.BlockSpec((1, D_MODEL), lambda i, j: (0, 0)),
            pl.BlockSpec((D_MODEL, tn), lambda i, j: (0, j)),
        ],
        out_specs=pl.BlockSpec((tm, tn), lambda i, j: (i, j)),
        out_shape=jax.ShapeDtypeStruct((n, IN_COLS), F32),
        scratch_shapes=[pltpu.VMEM((tm, D_MODEL), BF16)],
        compiler_params=pltpu.CompilerParams(
            dimension_semantics=("parallel", "arbitrary"), vmem_limit_bytes=VMEM_LIMIT),
        name="in_proj",
    )(h, gain, w)


def _delta_kernel(za_ref, sm_ref, convw_ref, rate_ref, dtb_ref, gain_ref, o_ref,
                  ext_ref, state_ref):
    c = pl.program_id(1)
    wqkv = 3 * D_MODEL

    @pl.when(c == 0)
    def _():
        state_ref[...] = jnp.zeros_like(state_ref)
        ext_ref[0:TAIL, :] = jnp.zeros((TAIL, wqkv), F32)

    cur = za_ref[:, 0:wqkv]
    ext_ref[TAIL:TAIL + CHUNK, :] = cur
    conv = convw_ref[0:1, :] * ext_ref[TAIL - 3:TAIL - 3 + CHUNK, :]
    for j in range(1, A_CONV):
        lo = TAIL - (A_CONV - 1) + j
        conv = conv + convw_ref[j:j + 1, :] * ext_ref[lo:lo + CHUNK, :]
    ext_ref[0:TAIL, :] = cur[CHUNK - TAIL:, :]
    qkv = _silu(conv)

    row, col, blocks = _tri_masks()
    incl = row >= col
    strict = row > col

    sm = sm_ref[:, S_BA:S_BA + 128]
    rows128 = lax.broadcasted_iota(jnp.int32, (CHUNK, 128), 0)
    real = jnp.logical_or(c != 0, rows128 >= PAD_ROWS)
    beta_all = jnp.where(real, _sigmoid(sm), 0.0)
    g_all = jnp.where(real, -rate_ref[...] * _softplus(sm + dtb_ref[...]), 0.0)
    gcum = _dot(incl.astype(F32), g_all)
    gcum_t = gcum.T

    for h in range(A_HEADS):
        lo = h * A_DK
        q = qkv[:, lo:lo + A_DK]
        k = qkv[:, D_MODEL + lo:D_MODEL + lo + A_DK]
        v = qkv[:, 2 * D_MODEL + lo:2 * D_MODEL + lo + A_DK]
        q = q * lax.rsqrt(jnp.sum(q * q, axis=-1, keepdims=True) + 1e-6) * (A_DK ** -0.5)
        k = k * lax.rsqrt(jnp.sum(k * k, axis=-1, keepdims=True) + 1e-6)
        beta = beta_all[:, h:h + 1]
        gc = gcum[:, A_HEADS + h:A_HEADS + h + 1]
        gr = gcum_t[A_HEADS + h:A_HEADS + h + 1, :]
        decay = jnp.where(incl, jnp.exp(jnp.where(incl, gc - gr, 0.0)), 0.0)
        kb = k * beta
        vb = v * beta
        m = jnp.where(strict, _dot_nt(kb, k) * decay, 0.0)
        tinv = _tri_inv(m, row, col, blocks)
        egc = jnp.exp(gc)
        uw = _dot(tinv, jnp.concatenate([vb, kb * egc], axis=1))
        attn = _dot_nt(q, k) * decay
        s = state_ref[h]
        ws_qs = _dot(jnp.concatenate([uw[:, A_DK:], q * egc], axis=0), s)
        v_new = uw[:, :A_DK] - ws_qs[:CHUNK]
        o = ws_qs[CHUNK:] + _dot(attn, v_new)
        g_last = gc[CHUNK - 1:CHUNK, :]
        k_tail = k * jnp.exp(g_last - gc)
        state_ref[h] = s * jnp.exp(g_last) + _dot_tn(k_tail, v_new)
        o = o * lax.rsqrt(jnp.mean(o * o, axis=-1, keepdims=True) + EPS) * gain_ref[...]
        zg = za_ref[:, 3 * D_MODEL + lo:3 * D_MODEL + lo + A_DK]
        o_ref[:, lo:lo + A_DK] = o * _silu(zg)


def _chunk_map(b, c):
    return (b, (c + 64) % 65, 0)


def _deltanet(z3, convw, rate, dtb, gain):
    bsz, tp, _ = z3.shape
    nc = tp // CHUNK
    return pl.pallas_call(
        _delta_kernel,
        grid=(bsz, nc),
        in_specs=[
            pl.BlockSpec((None, CHUNK, 4 * D_MODEL), lambda b, c: (b, (c + nc - 1) % nc, COL_A // (4 * D_MODEL))),
            pl.BlockSpec((None, CHUNK, 512), lambda b, c: (b, (c + nc - 1) % nc, COL_S // 512)),
            pl.BlockSpec((A_CONV, 3 * D_MODEL), lambda b, c: (0, 0)),
            pl.BlockSpec((1, 128), lambda b, c: (0, 0)),
            pl.BlockSpec((1, 128), lambda b, c: (0, 0)),
            pl.BlockSpec((1, A_DK), lambda b, c: (0, 0)),
        ],
        out_specs=pl.BlockSpec((None, CHUNK, D_MODEL), lambda b, c: (b, (c + nc - 1) % nc, 0)),
        out_shape=jax.ShapeDtypeStruct((bsz, tp, D_MODEL), F32),
        scratch_shapes=[
            pltpu.VMEM((TAIL + CHUNK, 3 * D_MODEL), F32),
            pltpu.VMEM((A_HEADS, A_DK, A_DK), F32),
        ],
        compiler_params=pltpu.CompilerParams(
            dimension_semantics=("parallel", "arbitrary"), vmem_limit_bytes=VMEM_LIMIT),
        name="deltanet",
    )(z3, z3, convw, rate, dtb, gain)


def _rwkv_kernel(zb_ref, sm_ref, mub_ref, mus_ref, w0_ref, wup_ref, a0_ref, aup_ref, gup_ref,
                 kk_ref, ka_ref, rk_ref, lng_ref, lnb_ref, o_ref,
                 extb_ref, exts_ref, state_ref):
    c = pl.program_id(1)
    wb = 3 * D_MODEL
    ws = 512 - S_WA

    @pl.when(c == 0)
    def _():
        state_ref[...] = jnp.zeros_like(state_ref)
        extb_ref[0:TAIL, :] = jnp.zeros((TAIL, wb), F32)
        exts_ref[0:TAIL, :] = jnp.zeros((TAIL, ws), F32)

    curb = zb_ref[...]
    extb_ref[TAIL:TAIL + CHUNK, :] = curb
    prevb = extb_ref[TAIL - 1:TAIL - 1 + CHUNK, :]
    extb_ref[0:TAIL, :] = curb[CHUNK - TAIL:, :]
    zb = curb + (prevb - curb) * mub_ref[...]
    curs = sm_ref[:, S_WA:512]
    exts_ref[TAIL:TAIL + CHUNK, :] = curs
    prevs = exts_ref[TAIL - 1:TAIL - 1 + CHUNK, :]
    exts_ref[0:TAIL, :] = curs[CHUNK - TAIL:, :]
    zs = curs + (prevs - curs) * mus_ref[...]

    r_all = zb[:, 0:D_MODEL]
    k_all = zb[:, D_MODEL:2 * D_MODEL]
    v_all = zb[:, 2 * D_MODEL:3 * D_MODEL]
    wa = zs[:, 0:128]
    gd = zs[:, 128:384]

    w_log = -_softplus(-(w0_ref[...] + _dot(jnp.tanh(wa), wup_ref[...]))) - 0.5
    logw_all = -jnp.exp(w_log)
    a_all = _sigmoid(a0_ref[...] + _dot(wa, aup_ref[...]))
    gate_all = _dot(_sigmoid(gd), gup_ref[...])
    kk_all = k_all * kk_ref[...]
    k2_all = k_all * (1.0 + (a_all - 1.0) * ka_ref[...])

    row, col, blocks = _tri_masks()
    incl = row >= col
    strict = row > col
    cum_all = _dot(incl.astype(F32), logw_all)

    for h in range(B_HEADS):
        sl = slice(h * B_N, (h + 1) * B_N)
        r = r_all[:, sl]
        k2 = k2_all[:, sl]
        v = v_all[:, sl]
        kk = kk_all[:, sl]
        kk = kk * lax.rsqrt(jnp.sum(kk * kk, axis=-1, keepdims=True) + 1e-6)
        av = -kk
        bv = kk * a_all[:, sl]
        cum = cum_all[:, sl]
        lw = logw_all[:, sl]
        c_last = cum[CHUNK - 1:CHUNK, :]
        p_inv = jnp.exp(-cum)
        p_tail = jnp.exp(c_last - cum)
        at_rt = jnp.concatenate([av * jnp.exp(cum - lw), r * jnp.exp(cum)], axis=0)
        bt_kt = jnp.concatenate([bv * p_inv, k2 * p_inv], axis=0)
        gram = _dot_nt(at_rt, bt_kt)
        a_ab = jnp.where(strict, gram[:CHUNK, :CHUNK], 0.0)
        a_ak = jnp.where(strict, gram[:CHUNK, CHUNK:], 0.0)
        a_rb = jnp.where(incl, gram[CHUNK:, :CHUNK], 0.0)
        a_rk = jnp.where(incl, gram[CHUNK:, CHUNK:], 0.0)
        tinv = _tri_inv(-a_ab, row, col, blocks)
        s = state_ref[h]
        x0 = _dot_nt(at_rt, s)
        u = _dot(tinv, x0[:CHUNK] + _dot(a_ak, v))
        uv = jnp.concatenate([u, v], axis=0)
        y = x0[CHUNK:] + _dot(jnp.concatenate([a_rb, a_rk], axis=1), uv)
        state_ref[h] = s * jnp.exp(c_last) + _dot_tn(
            uv, jnp.concatenate([bv * p_tail, k2 * p_tail], axis=0))
        mean = jnp.mean(y, axis=-1, keepdims=True)
        yc = y - mean
        var = jnp.mean(yc * yc, axis=-1, keepdims=True)
        yn = yc * lax.rsqrt(var + B_GN_EPS) * lng_ref[:, sl] + lnb_ref[:, sl]
        bonus = jnp.sum(r * k2 * rk_ref[:, sl], axis=-1, keepdims=True) * v
        o_ref[:, sl] = (yn + bonus) * gate_all[:, sl]


def _rwkv(z3, mub, mus, w0, wup, a0, aup, gup, kk, ka, rk, lng, lnb):
    bsz, tp, _ = z3.shape
    nc = tp // CHUNK
    vec = lambda w: pl.BlockSpec((1, w), lambda b, c: (0, 0))
    mat = lambda r: pl.BlockSpec((r, D_MODEL), lambda b, c: (0, 0))
    return pl.pallas_call(
        _rwkv_kernel,
        grid=(bsz, nc),
        in_specs=[
            pl.BlockSpec((None, CHUNK, 3 * D_MODEL), lambda b, c: (b, (c + nc - 1) % nc, COL_B // (3 * D_MODEL))),
            pl.BlockSpec((None, CHUNK, 512), lambda b, c: (b, (c + nc - 1) % nc, COL_S // 512)),
            vec(3 * D_MODEL), vec(512 - S_WA),
            vec(D_MODEL), mat(128), vec(D_MODEL), mat(128), mat(256),
            vec(D_MODEL), vec(D_MODEL), vec(D_MODEL), vec(D_MODEL), vec(D_MODEL),
        ],
        out_specs=pl.BlockSpec((None, CHUNK, D_MODEL), lambda b, c: (b, (c + nc - 1) % nc, 0)),
        out_shape=jax.ShapeDtypeStruct((bsz, tp, D_MODEL), F32),
        scratch_shapes=[
            pltpu.VMEM((TAIL + CHUNK, 3 * D_MODEL), F32),
            pltpu.VMEM((TAIL + CHUNK, 512 - S_WA), F32),
            pltpu.VMEM((B_HEADS, B_N, B_N), F32),
        ],
        compiler_params=pltpu.CompilerParams(
            dimension_semantics=("parallel", "arbitrary"), vmem_limit_bytes=VMEM_LIMIT),
        name="rwkv7",
    )(z3, z3, mub, mus, w0, wup, a0, aup, gup, kk, ka, rk, lng, lnb)


def _out_kernel(h_ref, g_ref, oa_ref, ob_ref, wout_ref, gain2_ref, wgu_ref, wd_ref, fgain_ref,
                o_ref, act_ref):
    merged = (_sigmoid(g_ref[:, 0:D_MODEL]) * oa_ref[...]
              + _sigmoid(g_ref[:, D_MODEL:2 * D_MODEL]) * ob_ref[...])
    h = h_ref[...] + jnp.dot(merged.astype(BF16), wout_ref[...], preferred_element_type=F32)
    xn = _rms(h, gain2_ref[...]).astype(BF16)
    h = h + 0.5 * _swiglu(xn, wgu_ref, wd_ref, act_ref)
    o_ref[...] = _rms(h, fgain_ref[...])


def _out_ffn2(h3, z3, oa, ob, wout, gain2, wgu, wd, fgain, seq, tm):
    bsz = h3.shape[0]
    const = dict(pipeline_mode=pl.Buffered(1))
    tile = lambda w, j: pl.BlockSpec((None, tm, w), lambda b, i: (b, i, j))
    return pl.pallas_call(
        _out_kernel,
        grid=(bsz, seq // tm),
        in_specs=[
            tile(D_MODEL, 0),
            tile(2 * D_MODEL, COL_G // (2 * D_MODEL)),
            tile(D_MODEL, 0),
            tile(D_MODEL, 0),
            pl.BlockSpec((D_MODEL, D_MODEL), lambda b, i: (0, 0), **const),
            pl.BlockSpec((1, D_MODEL), lambda b, i: (0, 0)),
            pl.BlockSpec((D_MODEL, 2 * D_FF), lambda b, i: (0, 0), **const),
            pl.BlockSpec((D_FF, D_MODEL), lambda b, i: (0, 0), **const),
            pl.BlockSpec((1, D_MODEL), lambda b, i: (0, 0)),
        ],
        out_specs=tile(D_MODEL, 0),
        out_shape=jax.ShapeDtypeStruct((bsz, seq, D_MODEL), F32),
        scratch_shapes=[pltpu.VMEM((tm, D_FF), BF16)],
        compiler_params=pltpu.CompilerParams(
            dimension_semantics=("parallel", "parallel"), vmem_limit_bytes=VMEM_LIMIT),
        name="out_ffn2",
    )(h3, z3, oa, ob, wout, gain2, wgu, wd, fgain)


def _regroup_in_weight(w):
    b0 = 4 * D_MODEL + 2 * A_HEADS
    lora0 = b0 + 3 * D_MODEL
    gate0 = lora0 + W_LORA + AA_LORA + G_LORA
    zeros = lambda n: jnp.zeros((w.shape[0], n), w.dtype)
    return jnp.concatenate([
        w[:, 0:4 * D_MODEL],
        w[:, gate0:gate0 + 2 * D_MODEL],
        w[:, b0:lora0],
        w[:, 4 * D_MODEL:b0], zeros(128 - 2 * A_HEADS),
        w[:, lora0:lora0 + W_LORA + AA_LORA],
        w[:, lora0 + W_LORA + AA_LORA:gate0], zeros(256 - G_LORA),
    ], axis=1)


def kernel(x, meta_tokens, ffn1_norm, ffn1_w_gu, ffn1_w_down, mix_norm, w_in, a_conv_w, a_log_rate, a_dt_bias, a_out_norm, b_shift_mu, b_w0, b_w_up, b_a0, b_a_up, b_g_up, b_k_k, b_k_a, b_r_k, b_ln_gain, b_ln_bias, w_out, ffn2_norm, ffn2_w_gu, ffn2_w_down, final_norm):
    bsz, seq, _ = x.shape
    tp = seq + CHUNK
    row = lambda v: v.reshape(1, -1).astype(F32)

    tail = jnp.concatenate([jnp.zeros((PAD_ROWS, D_MODEL), x.dtype), meta_tokens.astype(x.dtype)], axis=0)
    h = jnp.concatenate([x, jnp.broadcast_to(tail[None], (bsz, CHUNK, D_MODEL))], axis=1)
    h = h.reshape(bsz * tp, D_MODEL)

    l = 0
    h1 = _ffn1(h, row(ffn1_norm[l]), ffn1_w_gu[l].astype(BF16), ffn1_w_down[l].astype(BF16), tm=640)
    z = _inproj(h1, row(mix_norm[l]), _regroup_in_weight(w_in[l]).astype(BF16), tm=640, tn=2432)
    z3 = z.reshape(bsz, tp, IN_COLS)

    lane_pad = lambda v: jnp.pad(v.astype(F32), (A_HEADS, 128 - 2 * A_HEADS)).reshape(1, 128)
    o_a = _deltanet(z3, a_conv_w[l].astype(F32), lane_pad(jnp.exp(a_log_rate[l].astype(F32))),
                    lane_pad(a_dt_bias[l]), row(a_out_norm[l]))

    mu = b_shift_mu[l].astype(F32)
    mu_b = mu[:3 * D_MODEL].reshape(1, -1)
    mu_s = jnp.pad(mu[3 * D_MODEL:], (0, 256 - G_LORA)).reshape(1, -1)
    wup = jnp.pad(b_w_up[l].astype(F32), ((0, AA_LORA), (0, 0)))
    aup = jnp.pad(b_a_up[l].astype(F32), ((W_LORA, 0), (0, 0)))
    gup = jnp.pad(b_g_up[l].astype(F32), ((0, 256 - G_LORA), (0, 0)))
    o_b = _rwkv(z3, mu_b, mu_s, row(b_w0[l]), wup, row(b_a0[l]), aup, gup,
                row(b_k_k[l]), row(b_k_a[l]), row(b_r_k[l]), row(b_ln_gain[l]), row(b_ln_bias[l]))

    return _out_ffn2(h1.reshape(bsz, tp, D_MODEL), z3, o_a, o_b, w_out[l].astype(BF16),
                     row(ffn2_norm[l]), ffn2_w_gu[l].astype(BF16), ffn2_w_down[l].astype(BF16),
                     row(final_norm), seq=seq, tm=512)
```

```python
import jax
import jax.numpy as jnp
from jax import lax
from jax.experimental import pallas as pl
from jax.experimental.pallas import tpu as pltpu

F32 = jnp.float32
BF16 = jnp.bfloat16

D_MODEL = 1024
N_META = 16
EPS = 1e-6
D_FF = 2816
CHUNK = 64
A_DK = 128
A_HEADS = 8
A_CONV = 4
B_N = 64
B_HEADS = 16
W_LORA = 64
AA_LORA = 64
G_LORA = 160
B_GN_EPS = B_N * 1e-5

PAD_ROWS = CHUNK - N_META
FF_CHUNK = 256
TAIL = 8
PAIR = 2 * CHUNK
LANES = 128

COL_A = 0
COL_G = 4096
COL_B = 6144
COL_S = 9216
S_BA = 0
S_WA = 128
S_GD = 256
S_COLS = 512
IN_COLS = 9728

VMEM_LIMIT = 56 * 1024 * 1024

ONE, THREE, FULL = 1, 3, 6
NN = (((1,), (0,)), ((), ()))
NT = (((1,), (1,)), ((), ()))
TN = (((0,), (0,)), ((), ()))

MODE_LORA = THREE
MODE_GRAM = ONE
MODE_INV = ONE
MODE_APPLY = ONE
MODE_SREAD = ONE
MODE_SUPD = ONE


def _split(a):
    hi = a.astype(BF16)
    return hi, (a - hi.astype(F32)).astype(BF16)


def _mm(a, b, mode, dims=NN):
    dg = lambda x, y: lax.dot_general(x, y, dims, preferred_element_type=F32)
    if mode == FULL:
        return lax.dot_general(a, b, dims, precision=lax.Precision.HIGHEST,
                               preferred_element_type=F32)
    if mode == ONE:
        return dg(a.astype(BF16), b.astype(BF16))
    ah, al = _split(a)
    bh, bl = _split(b)
    return dg(ah, bh) + (dg(ah, bl) + dg(al, bh))


def _cumsum_rows(tri, x):
    t = tri.astype(BF16)
    hi = x.astype(BF16)
    r1 = x - hi.astype(F32)
    mid = r1.astype(BF16)
    lo = (r1 - mid.astype(F32)).astype(BF16)
    dg = lambda y: jnp.dot(t, y, preferred_element_type=F32)
    return dg(hi) + (dg(mid) + dg(lo))


def _sigmoid(x):
    return 1.0 / (1.0 + jnp.exp(-x))


def _silu(x):
    return x * _sigmoid(x)


def _softplus(x):
    return jnp.maximum(x, 0.0) + jnp.log(1.0 + jnp.exp(-jnp.abs(x)))


def _rms(x, gain):
    return x * lax.rsqrt(jnp.mean(x * x, axis=-1, keepdims=True) + EPS) * gain


def _pair_masks():
    row = lax.broadcasted_iota(jnp.int32, (PAIR, PAIR), 0)
    col = lax.broadcasted_iota(jnp.int32, (PAIR, PAIR), 1)
    same = (row >> 6) == (col >> 6)
    incl = jnp.logical_and(same, row >= col)
    strict = jnp.logical_and(same, row > col)
    eye = (row == col).astype(F32)
    blocks = [(row >> s) == (col >> s) for s in (3, 4, 5)]
    return same, incl, strict, eye, blocks


def _tri_inv(ms, eye, blocks):
    mm = lambda xs, ys: [_mm(x, y, MODE_INV) for x, y in zip(xs, ys)]
    n1 = [jnp.where(blocks[0], m, 0.0) for m in ms]
    n2 = mm(n1, n1)
    n4 = mm(n2, n2)
    d = mm([eye - a for a in n1], [eye + a for a in n2])
    d = mm(d, [eye + a for a in n4])
    inner = blocks[0]
    for outer in (blocks[1], blocks[2], None):
        off = jnp.logical_not(inner)
        if outer is not None:
            off = jnp.logical_and(outer, off)
        de = mm(d, [jnp.where(off, m, 0.0) for m in ms])
        d = [a - b for a, b in zip(d, mm(de, d))]
        inner = outer
    return d


def _stack(x, left):
    return jnp.concatenate([jnp.where(left, x, 0.0), jnp.where(left, 0.0, x)], axis=0)


def _fold(x):
    return x[:CHUNK] + x[CHUNK:]


def _head_sum(x, left):
    s0 = jnp.sum(jnp.where(left, x, 0.0), axis=-1, keepdims=True)
    s1 = jnp.sum(jnp.where(left, 0.0, x), axis=-1, keepdims=True)
    return jnp.where(left, s0, s1)


def _swiglu(xn, wgu_ref, wd_ref, act_ref):
    for c in range(D_FF // FF_CHUNK):
        lo = c * FF_CHUNK
        g = jnp.dot(xn, wgu_ref[:, lo:lo + FF_CHUNK], preferred_element_type=F32)
        u = jnp.dot(xn, wgu_ref[:, D_FF + lo:D_FF + lo + FF_CHUNK], preferred_element_type=F32)
        act_ref[:, lo:lo + FF_CHUNK] = (_silu(g) * u).astype(BF16)
    return jnp.dot(act_ref[...], wd_ref[...], preferred_element_type=F32)


def _ffn1_kernel(h_ref, gain_ref, wgu_ref, wd_ref, o_ref, act_ref):
    h = h_ref[...]
    xn = _rms(h, gain_ref[...]).astype(BF16)
    o_ref[...] = h + 0.5 * _swiglu(xn, wgu_ref, wd_ref, act_ref)


def _ffn1(h, gain, wgu, wd, tm):
    n = h.shape[0]
    const = dict(pipeline_mode=pl.Buffered(1))
    return pl.pallas_call(
        _ffn1_kernel,
        grid=(n // tm,),
        in_specs=[
            pl.BlockSpec((tm, D_MODEL), lambda i: (i, 0)),
            pl.BlockSpec((1, D_MODEL), lambda i: (0, 0)),
            pl.BlockSpec((D_MODEL, 2 * D_FF), lambda i: (0, 0), **const),
            pl.BlockSpec((D_FF, D_MODEL), lambda i: (0, 0), **const),
        ],
        out_specs=pl.BlockSpec((tm, D_MODEL), lambda i: (i, 0)),
        out_shape=jax.ShapeDtypeStruct((n, D_MODEL), F32),
        scratch_shapes=[pltpu.VMEM((tm, D_FF), BF16)],
        compiler_params=pltpu.CompilerParams(
            dimension_semantics=("parallel",), vmem_limit_bytes=VMEM_LIMIT),
        name="ffn1",
    )(h, gain, wgu, wd)


def _inproj_kernel(h_ref, gain_ref, w_ref, o_ref, u_ref):
    @pl.when(pl.program_id(1) == 0)
    def _():
        u_ref[...] = _rms(h_ref[...], gain_ref[...]).astype(BF16)

    o_ref[...] = jnp.dot(u_ref[...], w_ref[...], preferred_element_type=F32)


def _inproj(h, gain, w, tm, tn):
    n = h.shape[0]
    return pl.pallas_call(
        _inproj_kernel,
        grid=(n // tm, IN_COLS // tn),
        in_specs=[
            pl.BlockSpec((tm, D_MODEL), lambda i, j: (i, 0)),
            pl.BlockSpec((1, D_MODEL), lambda i, j: (0, 0)),
            pl.BlockSpec((D_MODEL, tn), lambda i, j: (0, j)),
        ],
        out_specs=pl.BlockSpec((tm, tn), lambda i, j: (i, j)),
        out_shape=jax.ShapeDtypeStruct((n, IN_COLS), F32),
        scratch_shapes=[pltpu.VMEM((tm, D_MODEL), BF16)],
        compiler_params=pltpu.CompilerParams(
            dimension_semantics=("parallel", "arbitrary"), vmem_limit_bytes=VMEM_LIMIT),
        name="in_proj",
    )(h, gain, w)


def _delta_kernel(za_ref, sm_ref, convw_ref, rate_ref, dtb_ref, gain_ref, o_ref,
                  ext_ref, state_ref):
    c = pl.program_id(1)
    wqkv = 3 * D_MODEL

    @pl.when(c == 0)
    def _():
        state_ref[...] = jnp.zeros_like(state_ref)
        ext_ref[0:TAIL, :] = jnp.zeros((TAIL, wqkv), F32)

    cur = za_ref[:, 0:wqkv]
    ext_ref[TAIL:TAIL + CHUNK, :] = cur
    conv = convw_ref[0:1, :] * ext_ref[TAIL - 3:TAIL - 3 + CHUNK, :]
    for j in range(1, A_CONV):
        lo = TAIL - (A_CONV - 1) + j
        conv = conv + convw_ref[j:j + 1, :] * ext_ref[lo:lo + CHUNK, :]
    ext_ref[0:TAIL, :] = cur[CHUNK - TAIL:, :]
    qkv = _silu(conv)

    same, incl, strict, eye, blocks = _pair_masks()

    sm = sm_ref[:, S_BA:S_BA + LANES]
    rows = lax.broadcasted_iota(jnp.int32, (CHUNK, LANES), 0)
    real = jnp.logical_or(c != 0, rows >= PAD_ROWS)
    beta_all = jnp.where(real, _sigmoid(sm), 0.0)
    g_all = jnp.where(real, -rate_ref[...] * _softplus(sm + dtb_ref[...]), 0.0)
    tri = (lax.broadcasted_iota(jnp.int32, (CHUNK, CHUNK), 0)
           >= lax.broadcasted_iota(jnp.int32, (CHUNK, CHUNK), 1))
    gcum = _cumsum_rows(tri, g_all)
    gcum_t = gcum.T
    zero = jnp.zeros((CHUNK, A_DK), F32)

    def diag2(x):
        return jnp.concatenate([jnp.concatenate([x[:CHUNK], zero], axis=1),
                                jnp.concatenate([zero, x[CHUNK:]], axis=1)], axis=0)

    pairs = range(A_HEADS // 2)
    gc, g_last, decay, k_st, q_st, kb_st, vb_st, egc = ([] for _ in range(8))
    for p in pairs:
        qs, ks, vs, betas = [], [], [], []
        for h in (2 * p, 2 * p + 1):
            lo = h * A_DK
            q = qkv[:, lo:lo + A_DK]
            k = qkv[:, D_MODEL + lo:D_MODEL + lo + A_DK]
            qs.append(q * lax.rsqrt(jnp.sum(q * q, axis=-1, keepdims=True) + 1e-6) * (A_DK ** -0.5))
            ks.append(k * lax.rsqrt(jnp.sum(k * k, axis=-1, keepdims=True) + 1e-6))
            vs.append(qkv[:, 2 * D_MODEL + lo:2 * D_MODEL + lo + A_DK])
            betas.append(beta_all[:, h:h + 1])
        la, lb = A_HEADS + 2 * p, A_HEADS + 2 * p + 1
        gc_p = jnp.concatenate([gcum[:, la:la + 1], gcum[:, lb:lb + 1]], axis=0)
        gr_p = jnp.concatenate([gcum_t[la:la + 1, :], gcum_t[lb:lb + 1, :]], axis=1)
        g_last.append(jnp.concatenate([jnp.broadcast_to(gc_p[CHUNK - 1:CHUNK, :], (CHUNK, 1)),
                                       jnp.broadcast_to(gc_p[PAIR - 1:PAIR, :], (CHUNK, 1))], axis=0))
        decay.append(jnp.where(incl, jnp.exp(jnp.where(incl, gc_p - gr_p, 0.0)), 0.0))
        beta = jnp.concatenate(betas, axis=0)
        k_st.append(jnp.concatenate(ks, axis=0))
        q_st.append(jnp.concatenate(qs, axis=0))
        kb_st.append(k_st[p] * beta)
        vb_st.append(jnp.concatenate(vs, axis=0) * beta)
        gc.append(gc_p)
        egc.append(jnp.exp(gc_p))

    gram = [_mm(jnp.concatenate([diag2(kb_st[p]), diag2(q_st[p])], axis=0), diag2(k_st[p]), MODE_GRAM, NT)
            for p in pairs]
    attn = [gram[p][PAIR:] * decay[p] for p in pairs]
    tinv = _tri_inv([jnp.where(strict, gram[p][:PAIR] * decay[p], 0.0) for p in pairs], eye, blocks)
    uw = [_mm(tinv[p], jnp.concatenate([vb_st[p], kb_st[p] * egc[p]], axis=1), MODE_APPLY)
          for p in pairs]
    ws_qs = [_mm(jnp.concatenate([diag2(uw[p][:, A_DK:]), diag2(q_st[p] * egc[p])], axis=0),
                 jnp.concatenate([state_ref[2 * p], state_ref[2 * p + 1]], axis=0), MODE_SREAD)
             for p in pairs]
    v_new = [uw[p][:, :A_DK] - ws_qs[p][:PAIR] for p in pairs]
    av = [_mm(attn[p], v_new[p], MODE_APPLY) for p in pairs]
    upd = [_mm(k_st[p] * jnp.exp(g_last[p] - gc[p]), diag2(v_new[p]), MODE_SUPD, TN)
           for p in pairs]
    for p in pairs:
        e_last = jnp.exp(g_last[p])
        state_ref[2 * p] = state_ref[2 * p] * e_last[0:1, :] + upd[p][:, :A_DK]
        state_ref[2 * p + 1] = state_ref[2 * p + 1] * e_last[CHUNK:CHUNK + 1, :] + upd[p][:, A_DK:]
        o = ws_qs[p][PAIR:] + av[p]
        o = o * lax.rsqrt(jnp.mean(o * o, axis=-1, keepdims=True) + EPS) * gain_ref[...]
        for i, h in enumerate((2 * p, 2 * p + 1)):
            lo = h * A_DK
            zg = za_ref[:, 3 * D_MODEL + lo:3 * D_MODEL + lo + A_DK]
            o_ref[:, lo:lo + A_DK] = o[i * CHUNK:(i + 1) * CHUNK] * _silu(zg)


def _deltanet(z3, convw, rate, dtb, gain):
    bsz, tp, _ = z3.shape
    nc = tp // CHUNK
    return pl.pallas_call(
        _delta_kernel,
        grid=(bsz, nc),
        in_specs=[
            pl.BlockSpec((None, CHUNK, 4 * D_MODEL), lambda b, c: (b, (c + nc - 1) % nc, COL_A // (4 * D_MODEL))),
            pl.BlockSpec((None, CHUNK, S_COLS), lambda b, c: (b, (c + nc - 1) % nc, COL_S // S_COLS)),
            pl.BlockSpec((A_CONV, 3 * D_MODEL), lambda b, c: (0, 0)),
            pl.BlockSpec((1, LANES), lambda b, c: (0, 0)),
            pl.BlockSpec((1, LANES), lambda b, c: (0, 0)),
            pl.BlockSpec((1, A_DK), lambda b, c: (0, 0)),
        ],
        out_specs=pl.BlockSpec((None, CHUNK, D_MODEL), lambda b, c: (b, (c + nc - 1) % nc, 0)),
        out_shape=jax.ShapeDtypeStruct((bsz, tp, D_MODEL), F32),
        scratch_shapes=[
            pltpu.VMEM((TAIL + CHUNK, 3 * D_MODEL), F32),
            pltpu.VMEM((A_HEADS, A_DK, A_DK), F32),
        ],
        compiler_params=pltpu.CompilerParams(
            dimension_semantics=("parallel", "arbitrary"), vmem_limit_bytes=VMEM_LIMIT),
        name="deltanet",
    )(z3, z3, convw, rate, dtb, gain)


def _rwkv_kernel(zb_ref, sm_ref, mub_ref, mus_ref, w0_ref, wup_ref, a0_ref, aup_ref, gup_ref,
                 kk_ref, ka_ref, rk_ref, lng_ref, lnb_ref, o_ref,
                 extb_ref, exts_ref, state_ref):
    c = pl.program_id(1)
    wb = 3 * D_MODEL
    ws = S_COLS - S_WA

    @pl.when(c == 0)
    def _():
        state_ref[...] = jnp.zeros_like(state_ref)
        extb_ref[0:TAIL, :] = jnp.zeros((TAIL, wb), F32)
        exts_ref[0:TAIL, :] = jnp.zeros((TAIL, ws), F32)

    curb = zb_ref[...]
    extb_ref[TAIL:TAIL + CHUNK, :] = curb
    prevb = extb_ref[TAIL - 1:TAIL - 1 + CHUNK, :]
    extb_ref[0:TAIL, :] = curb[CHUNK - TAIL:, :]
    zb = curb + (prevb - curb) * mub_ref[...]
    curs = sm_ref[:, S_WA:S_COLS]
    exts_ref[TAIL:TAIL + CHUNK, :] = curs
    prevs = exts_ref[TAIL - 1:TAIL - 1 + CHUNK, :]
    exts_ref[0:TAIL, :] = curs[CHUNK - TAIL:, :]
    zs = curs + (prevs - curs) * mus_ref[...]

    r_all = zb[:, 0:D_MODEL]
    k_all = zb[:, D_MODEL:2 * D_MODEL]
    v_all = zb[:, 2 * D_MODEL:3 * D_MODEL]
    wa = zs[:, 0:LANES]
    gd = zs[:, LANES:3 * LANES]

    w_log = -_softplus(-(w0_ref[...] + _mm(jnp.tanh(wa), wup_ref[...], MODE_LORA))) - 0.5
    logw_all = -jnp.exp(w_log)
    a_all = _sigmoid(a0_ref[...] + _mm(wa, aup_ref[...], MODE_LORA))
    gate_all = _mm(_sigmoid(gd), gup_ref[...], MODE_LORA)
    kk_all = k_all * kk_ref[...]
    k2_all = k_all * (1.0 + (a_all - 1.0) * ka_ref[...])

    same, incl, strict, eye, blocks = _pair_masks()
    left = lax.broadcasted_iota(jnp.int32, (CHUNK, LANES), 1) < B_N
    tri = (lax.broadcasted_iota(jnp.int32, (CHUNK, CHUNK), 0)
           >= lax.broadcasted_iota(jnp.int32, (CHUNK, CHUNK), 1))
    cum_all = _cumsum_rows(tri, logw_all)

    pairs = range(B_HEADS // 2)
    lanes = [slice(p * LANES, (p + 1) * LANES) for p in pairs]
    r, k2, v, bv, c_last, p_tail, atrt, v_st, gram = ([] for _ in range(9))
    for p, sl in enumerate(lanes):
        kk = kk_all[:, sl]
        kk = kk * lax.rsqrt(_head_sum(kk * kk, left) + 1e-6)
        cum = cum_all[:, sl]
        r.append(r_all[:, sl])
        k2.append(k2_all[:, sl])
        v.append(v_all[:, sl])
        bv.append(kk * a_all[:, sl])
        c_last.append(cum[CHUNK - 1:CHUNK, :])
        p_inv = jnp.exp(-cum)
        p_tail.append(jnp.exp(c_last[p] - cum))
        at = -kk * jnp.exp(cum - logw_all[:, sl])
        rt = r[p] * jnp.exp(cum)
        atrt.append(jnp.concatenate([at, rt], axis=0))
        v_st.append(_stack(v[p], left))
        gram.append(_mm(jnp.concatenate([_stack(at, left), _stack(rt, left)], axis=0),
                        jnp.concatenate([_stack(bv[p] * p_inv, left), _stack(k2[p] * p_inv, left)], axis=0),
                        MODE_GRAM, NT))
    x0 = [_mm(atrt[p], state_ref[p], MODE_SREAD, NT) for p in pairs]
    akv = [_mm(jnp.where(strict, gram[p][:PAIR, PAIR:], 0.0), v_st[p], MODE_APPLY) for p in pairs]
    tinv = _tri_inv([jnp.where(strict, -gram[p][:PAIR, :PAIR], 0.0) for p in pairs], eye, blocks)
    u = [_fold(_mm(tinv[p], _stack(x0[p][:CHUNK] + _fold(akv[p]), left), MODE_APPLY)) for p in pairs]
    yy = [_mm(jnp.where(jnp.concatenate([incl, incl], axis=1), gram[p][PAIR:, :], 0.0),
              jnp.concatenate([_stack(u[p], left), v_st[p]], axis=0), MODE_APPLY) for p in pairs]
    upd = [_mm(jnp.concatenate([u[p], v[p]], axis=0),
               jnp.concatenate([bv[p] * p_tail[p], k2[p] * p_tail[p]], axis=0), MODE_SUPD, TN)
           for p in pairs]
    for p, sl in enumerate(lanes):
        state_ref[p] = state_ref[p] * jnp.exp(c_last[p]) + jnp.where(same, upd[p], 0.0)
        y = x0[p][CHUNK:] + _fold(yy[p])
        mean = _head_sum(y, left) * (1.0 / B_N)
        yc = y - mean
        var = _head_sum(yc * yc, left) * (1.0 / B_N)
        yn = yc * lax.rsqrt(var + B_GN_EPS) * lng_ref[:, sl] + lnb_ref[:, sl]
        bonus = _head_sum(r[p] * k2[p] * rk_ref[:, sl], left) * v[p]
        o_ref[:, sl] = (yn + bonus) * gate_all[:, sl]


def _rwkv(z3, mub, mus, w0, wup, a0, aup, gup, kk, ka, rk, lng, lnb):
    bsz, tp, _ = z3.shape
    nc = tp // CHUNK
    vec = lambda w: pl.BlockSpec((1, w), lambda b, c: (0, 0))
    mat = lambda r: pl.BlockSpec((r, D_MODEL), lambda b, c: (0, 0))
    return pl.pallas_call(
        _rwkv_kernel,
        grid=(bsz, nc),
        in_specs=[
            pl.BlockSpec((None, CHUNK, 3 * D_MODEL), lambda b, c: (b, (c + nc - 1) % nc, COL_B // (3 * D_MODEL))),
            pl.BlockSpec((None, CHUNK, S_COLS), lambda b, c: (b, (c + nc - 1) % nc, COL_S // S_COLS)),
            vec(3 * D_MODEL), vec(S_COLS - S_WA),
            vec(D_MODEL), mat(LANES), vec(D_MODEL), mat(LANES), mat(2 * LANES),
            vec(D_MODEL), vec(D_MODEL), vec(D_MODEL), vec(D_MODEL), vec(D_MODEL),
        ],
        out_specs=pl.BlockSpec((None, CHUNK, D_MODEL), lambda b, c: (b, (c + nc - 1) % nc, 0)),
        out_shape=jax.ShapeDtypeStruct((bsz, tp, D_MODEL), F32),
        scratch_shapes=[
            pltpu.VMEM((TAIL + CHUNK, 3 * D_MODEL), F32),
            pltpu.VMEM((TAIL + CHUNK, S_COLS - S_WA), F32),
            pltpu.VMEM((B_HEADS // 2, PAIR, PAIR), F32),
        ],
        compiler_params=pltpu.CompilerParams(
            dimension_semantics=("parallel", "arbitrary"), vmem_limit_bytes=VMEM_LIMIT),
        name="rwkv7",
    )(z3, z3, mub, mus, w0, wup, a0, aup, gup, kk, ka, rk, lng, lnb)


def _out_kernel(h_ref, g_ref, oa_ref, ob_ref, wout_ref, gain2_ref, wgu_ref, wd_ref, fgain_ref,
                o_ref, act_ref):
    merged = (_sigmoid(g_ref[:, 0:D_MODEL]) * oa_ref[...]
              + _sigmoid(g_ref[:, D_MODEL:2 * D_MODEL]) * ob_ref[...])
    h = h_ref[...] + jnp.dot(merged.astype(BF16), wout_ref[...], preferred_element_type=F32)
    xn = _rms(h, gain2_ref[...]).astype(BF16)
    h = h + 0.5 * _swiglu(xn, wgu_ref, wd_ref, act_ref)
    o_ref[...] = _rms(h, fgain_ref[...])


def _out_ffn2(h3, z3, oa, ob, wout, gain2, wgu, wd, fgain, seq, tm):
    bsz = h3.shape[0]
    const = dict(pipeline_mode=pl.Buffered(1))
    tile = lambda w, j: pl.BlockSpec((None, tm, w), lambda b, i: (b, i, j))
    return pl.pallas_call(
        _out_kernel,
        grid=(bsz, seq // tm),
        in_specs=[
            tile(D_MODEL, 0),
            tile(2 * D_MODEL, COL_G // (2 * D_MODEL)),
            tile(D_MODEL, 0),
            tile(D_MODEL, 0),
            pl.BlockSpec((D_MODEL, D_MODEL), lambda b, i: (0, 0), **const),
            pl.BlockSpec((1, D_MODEL), lambda b, i: (0, 0)),
            pl.BlockSpec((D_MODEL, 2 * D_FF), lambda b, i: (0, 0), **const),
            pl.BlockSpec((D_FF, D_MODEL), lambda b, i: (0, 0), **const),
            pl.BlockSpec((1, D_MODEL), lambda b, i: (0, 0)),
        ],
        out_specs=tile(D_MODEL, 0),
        out_shape=jax.ShapeDtypeStruct((bsz, seq, D_MODEL), F32),
        scratch_shapes=[pltpu.VMEM((tm, D_FF), BF16)],
        compiler_params=pltpu.CompilerParams(
            dimension_semantics=("parallel", "parallel"), vmem_limit_bytes=VMEM_LIMIT),
        name="out_ffn2",
    )(h3, z3, oa, ob, wout, gain2, wgu, wd, fgain)


def _regroup_in_weight(w):
    b0 = 4 * D_MODEL + 2 * A_HEADS
    lora0 = b0 + 3 * D_MODEL
    gate0 = lora0 + W_LORA + AA_LORA + G_LORA
    zeros = lambda n: jnp.zeros((w.shape[0], n), w.dtype)
    return jnp.concatenate([
        w[:, 0:4 * D_MODEL],
        w[:, gate0:gate0 + 2 * D_MODEL],
        w[:, b0:lora0],
        w[:, 4 * D_MODEL:b0], zeros(LANES - 2 * A_HEADS),
        w[:, lora0:lora0 + W_LORA + AA_LORA],
        w[:, lora0 + W_LORA + AA_LORA:gate0], zeros(2 * LANES - G_LORA),
    ], axis=1)


def kernel(x, meta_tokens, ffn1_norm, ffn1_w_gu, ffn1_w_down, mix_norm, w_in, a_conv_w, a_log_rate, a_dt_bias, a_out_norm, b_shift_mu, b_w0, b_w_up, b_a0, b_a_up, b_g_up, b_k_k, b_k_a, b_r_k, b_ln_gain, b_ln_bias, w_out, ffn2_norm, ffn2_w_gu, ffn2_w_down, final_norm):
    bsz, seq, _ = x.shape
    tp = seq + CHUNK
    row = lambda v: v.reshape(1, -1).astype(F32)

    tail = jnp.concatenate([jnp.zeros((PAD_ROWS, D_MODEL), x.dtype), meta_tokens.astype(x.dtype)], axis=0)
    h = jnp.concatenate([x, jnp.broadcast_to(tail[None], (bsz, CHUNK, D_MODEL))], axis=1)
    h = h.reshape(bsz * tp, D_MODEL)

    l = 0
    h1 = _ffn1(h, row(ffn1_norm[l]), ffn1_w_gu[l].astype(BF16), ffn1_w_down[l].astype(BF16), tm=640)
    z = _inproj(h1, row(mix_norm[l]), _regroup_in_weight(w_in[l]).astype(BF16), tm=640, tn=2432)
    z3 = z.reshape(bsz, tp, IN_COLS)

    lane_pad = lambda v: jnp.pad(v.astype(F32), (A_HEADS, LANES - 2 * A_HEADS)).reshape(1, LANES)
    o_a = _deltanet(z3, a_conv_w[l].astype(F32), lane_pad(jnp.exp(a_log_rate[l].astype(F32))),
                    lane_pad(a_dt_bias[l]), row(a_out_norm[l]))

    mu = b_shift_mu[l].astype(F32)
    mu_b = mu[:3 * D_MODEL].reshape(1, -1)
    mu_s = jnp.pad(mu[3 * D_MODEL:], (0, 2 * LANES - G_LORA)).reshape(1, -1)
    wup = jnp.pad(b_w_up[l].astype(F32), ((0, AA_LORA), (0, 0)))
    aup = jnp.pad(b_a_up[l].astype(F32), ((W_LORA, 0), (0, 0)))
    gup = jnp.pad(b_g_up[l].astype(F32), ((0, 2 * LANES - G_LORA), (0, 0)))
    o_b = _rwkv(z3, mu_b, mu_s, row(b_w0[l]), wup, row(b_a0[l]), aup, gup,
                row(b_k_k[l]), row(b_k_a[l]), row(b_r_k[l]), row(b_ln_gain[l]), row(b_ln_bias[l]))

    return _out_ffn2(h1.reshape(bsz, tp, D_MODEL), z3, o_a, o_b, w_out[l].astype(BF16),
                     row(ffn2_norm[l]), ffn2_w_gu[l].astype(BF16), ffn2_w_down[l].astype(BF16),
                     row(final_norm), seq=seq, tm=512)
```

```python
import jax
import jax.numpy as jnp
from jax import lax
from jax.experimental import pallas as pl
from jax.experimental.pallas import tpu as pltpu

F32 = jnp.float32
BF16 = jnp.bfloat16

D_MODEL = 1024
N_META = 16
EPS = 1e-6
D_FF = 2816
CHUNK = 64
A_DK = 128
A_HEADS = 8
A_CONV = 4
B_N = 64
B_HEADS = 16
W_LORA = 64
AA_LORA = 64
G_LORA = 160
B_GN_EPS = B_N * 1e-5

PAD_ROWS = CHUNK - N_META
FF_CHUNK = 256
TAIL = 8
PAIR = 2 * CHUNK
LANES = 128

COL_A = 0
COL_G = 4096
COL_B = 6144
COL_S = 9216
S_BA = 0
S_WA = 128
S_GD = 256
S_COLS = 512
IN_COLS = 9728

VMEM_LIMIT = 56 * 1024 * 1024

ONE, THREE, FULL = 1, 3, 6
NN = (((1,), (0,)), ((), ()))
NT = (((1,), (1,)), ((), ()))
TN = (((0,), (0,)), ((), ()))

MODE_LORA = ONE
MODE_GRAM = ONE
MODE_INV = ONE
MODE_APPLY = ONE
MODE_SREAD = ONE
MODE_SUPD = ONE


def _split(a):
    hi = a.astype(BF16)
    return hi, (a - hi.astype(F32)).astype(BF16)


def _mm(a, b, mode, dims=NN):
    dg = lambda x, y: lax.dot_general(x, y, dims, preferred_element_type=F32)
    if mode == FULL:
        return lax.dot_general(a, b, dims, precision=lax.Precision.HIGHEST,
                               preferred_element_type=F32)
    if mode == ONE:
        return dg(a.astype(BF16), b.astype(BF16))
    ah, al = _split(a)
    bh, bl = _split(b)
    return dg(ah, bh) + (dg(ah, bl) + dg(al, bh))


def _cumsum_rows(tri, x):
    t = tri.astype(BF16)
    hi = x.astype(BF16)
    r1 = x - hi.astype(F32)
    mid = r1.astype(BF16)
    lo = (r1 - mid.astype(F32)).astype(BF16)
    dg = lambda y: jnp.dot(t, y, preferred_element_type=F32)
    return dg(hi) + (dg(mid) + dg(lo))


def _sigmoid(x):
    return 1.0 / (1.0 + jnp.exp(-x))


def _silu(x):
    return x * _sigmoid(x)


def _softplus(x):
    return jnp.maximum(x, 0.0) + jnp.log(1.0 + jnp.exp(-jnp.abs(x)))


def _rms(x, gain):
    return x * lax.rsqrt(jnp.mean(x * x, axis=-1, keepdims=True) + EPS) * gain


def _pair_masks():
    row = lax.broadcasted_iota(jnp.int32, (PAIR, PAIR), 0)
    col = lax.broadcasted_iota(jnp.int32, (PAIR, PAIR), 1)
    same = (row >> 6) == (col >> 6)
    incl = jnp.logical_and(same, row >= col)
    strict = jnp.logical_and(same, row > col)
    eye = (row == col).astype(F32)
    blocks = [(row >> s) == (col >> s) for s in (3, 4, 5)]
    return same, incl, strict, eye, blocks


def _tri_inv(ms, eye, blocks):
    mm = lambda xs, ys: [_mm(x, y, MODE_INV) for x, y in zip(xs, ys)]
    pre = (lambda xs: [x.astype(BF16) for x in xs]) if MODE_INV == ONE else (lambda xs: xs)
    n1 = pre([jnp.where(blocks[0], m, 0.0) for m in ms])
    n2 = pre(mm(n1, n1))
    n4 = mm(n2, n2)
    one = eye.astype(n1[0].dtype)
    d = mm([one - a for a in n1], [one + a for a in n2])
    d = mm(d, [eye + a for a in n4])
    inner = blocks[0]
    for outer in (blocks[1], blocks[2], None):
        off = jnp.logical_not(inner)
        if outer is not None:
            off = jnp.logical_and(outer, off)
        dn = pre(d)
        de = mm(dn, [jnp.where(off, m, 0.0) for m in ms])
        d = [a - b for a, b in zip(d, mm(de, dn))]
        inner = outer
    return d


def _stack(x, left):
    return jnp.concatenate([jnp.where(left, x, 0.0), jnp.where(left, 0.0, x)], axis=0)


def _fold(x):
    return x[:CHUNK] + x[CHUNK:]


def _head_sum(x, left):
    s0 = jnp.sum(jnp.where(left, x, 0.0), axis=-1, keepdims=True)
    s1 = jnp.sum(jnp.where(left, 0.0, x), axis=-1, keepdims=True)
    return jnp.where(left, s0, s1)


def _swiglu(xn, wgu_ref, wd_ref, act_ref):
    for c in range(D_FF // FF_CHUNK):
        lo = c * FF_CHUNK
        g = jnp.dot(xn, wgu_ref[:, lo:lo + FF_CHUNK], preferred_element_type=F32)
        u = jnp.dot(xn, wgu_ref[:, D_FF + lo:D_FF + lo + FF_CHUNK], preferred_element_type=F32)
        act_ref[:, lo:lo + FF_CHUNK] = (_silu(g) * u).astype(BF16)
    return jnp.dot(act_ref[...], wd_ref[...], preferred_element_type=F32)


def _ffn1_kernel(h_ref, gain_ref, wgu_ref, wd_ref, mixgain_ref, o_ref, u_ref, act_ref):
    h = h_ref[...]
    xn = _rms(h, gain_ref[...]).astype(BF16)
    h = h + 0.5 * _swiglu(xn, wgu_ref, wd_ref, act_ref)
    o_ref[...] = h
    u_ref[...] = _rms(h, mixgain_ref[...]).astype(BF16)


def _ffn1(h, gain, wgu, wd, mixgain, tm):
    n = h.shape[0]
    const = dict(pipeline_mode=pl.Buffered(1))
    tile = pl.BlockSpec((tm, D_MODEL), lambda i: (i, 0))
    vec = pl.BlockSpec((1, D_MODEL), lambda i: (0, 0))
    return pl.pallas_call(
        _ffn1_kernel,
        grid=(n // tm,),
        in_specs=[
            tile, vec,
            pl.BlockSpec((D_MODEL, 2 * D_FF), lambda i: (0, 0), **const),
            pl.BlockSpec((D_FF, D_MODEL), lambda i: (0, 0), **const),
            vec,
        ],
        out_specs=[tile, tile],
        out_shape=[jax.ShapeDtypeStruct((n, D_MODEL), F32), jax.ShapeDtypeStruct((n, D_MODEL), BF16)],
        scratch_shapes=[pltpu.VMEM((tm, D_FF), BF16)],
        compiler_params=pltpu.CompilerParams(
            dimension_semantics=("parallel",), vmem_limit_bytes=VMEM_LIMIT),
        name="ffn1",
    )(h, gain, wgu, wd, mixgain)


def _inproj_kernel(u_ref, w_ref, o_ref):
    o_ref[...] = jnp.dot(u_ref[...], w_ref[...], preferred_element_type=F32)


def _inproj(u, w, tm, tn):
    n = u.shape[0]
    return pl.pallas_call(
        _inproj_kernel,
        grid=(IN_COLS // tn, n // tm),
        in_specs=[
            pl.BlockSpec((tm, D_MODEL), lambda j, i: (i, 0)),
            pl.BlockSpec((D_MODEL, tn), lambda j, i: (0, j)),
        ],
        out_specs=pl.BlockSpec((tm, tn), lambda j, i: (i, j)),
        out_shape=jax.ShapeDtypeStruct((n, IN_COLS), F32),
        compiler_params=pltpu.CompilerParams(
            dimension_semantics=("parallel", "parallel"), vmem_limit_bytes=VMEM_LIMIT),
        name="in_proj",
    )(u, w)


def _mixer_kernel(za_ref, zb_ref, sm_ref,
                  convw_ref, rate_ref, dtb_ref, again_ref,
                  mub_ref, mus_ref, w0_ref, wup_ref, a0_ref, aup_ref, gup_ref,
                  kk_ref, ka_ref, rk_ref, lng_ref, lnb_ref,
                  oa_ref, ob_ref,
                  exta_ref, extb_ref, exts_ref, sa_ref, sb_ref):
    c = pl.program_id(1)
    wqkv = 3 * D_MODEL
    ws = S_COLS - S_WA

    @pl.when(c == 0)
    def _():
        sa_ref[...] = jnp.zeros_like(sa_ref)
        sb_ref[...] = jnp.zeros_like(sb_ref)
        exta_ref[0:TAIL, :] = jnp.zeros((TAIL, wqkv), F32)
        extb_ref[0:TAIL, :] = jnp.zeros((TAIL, wqkv), F32)
        exts_ref[0:TAIL, :] = jnp.zeros((TAIL, ws), F32)

    same, incl, strict, eye, blocks = _pair_masks()
    left = lax.broadcasted_iota(jnp.int32, (CHUNK, LANES), 1) < B_N
    tri = (lax.broadcasted_iota(jnp.int32, (CHUNK, CHUNK), 0)
           >= lax.broadcasted_iota(jnp.int32, (CHUNK, CHUNK), 1))
    apairs = range(A_HEADS // 2)
    bpairs = range(B_HEADS // 2)
    blanes = [slice(p * LANES, (p + 1) * LANES) for p in bpairs]

    curb = zb_ref[...]
    extb_ref[TAIL:TAIL + CHUNK, :] = curb
    prevb = extb_ref[TAIL - 1:TAIL - 1 + CHUNK, :]
    extb_ref[0:TAIL, :] = curb[CHUNK - TAIL:, :]
    zb = curb + (prevb - curb) * mub_ref[...]
    curs = sm_ref[:, S_WA:S_COLS]
    exts_ref[TAIL:TAIL + CHUNK, :] = curs
    prevs = exts_ref[TAIL - 1:TAIL - 1 + CHUNK, :]
    exts_ref[0:TAIL, :] = curs[CHUNK - TAIL:, :]
    zs = curs + (prevs - curs) * mus_ref[...]

    r_all = zb[:, 0:D_MODEL]
    k_all = zb[:, D_MODEL:2 * D_MODEL]
    v_all = zb[:, 2 * D_MODEL:3 * D_MODEL]
    wa = zs[:, 0:LANES]
    gd = zs[:, LANES:3 * LANES]

    w_log = -_softplus(-(w0_ref[...] + _mm(jnp.tanh(wa), wup_ref[...], MODE_LORA))) - 0.5
    logw_all = -jnp.exp(w_log)
    a_all = _sigmoid(a0_ref[...] + _mm(wa, aup_ref[...], MODE_LORA))
    gate_all = _mm(_sigmoid(gd), gup_ref[...], MODE_LORA)
    kk_all = k_all * kk_ref[...]
    k2_all = k_all * (1.0 + (a_all - 1.0) * ka_ref[...])
    cum_all = _cumsum_rows(tri, logw_all)

    cura = za_ref[:, 0:wqkv]
    exta_ref[TAIL:TAIL + CHUNK, :] = cura
    conv = convw_ref[0:1, :] * exta_ref[TAIL - 3:TAIL - 3 + CHUNK, :]
    for j in range(1, A_CONV):
        lo = TAIL - (A_CONV - 1) + j
        conv = conv + convw_ref[j:j + 1, :] * exta_ref[lo:lo + CHUNK, :]
    exta_ref[0:TAIL, :] = cura[CHUNK - TAIL:, :]
    qkv = _silu(conv)

    sm = sm_ref[:, S_BA:S_BA + LANES]
    rows = lax.broadcasted_iota(jnp.int32, (CHUNK, LANES), 0)
    real = jnp.logical_or(c != 0, rows >= PAD_ROWS)
    beta_all = jnp.where(real, _sigmoid(sm), 0.0)
    g_all = jnp.where(real, -rate_ref[...] * _softplus(sm + dtb_ref[...]), 0.0)
    gcum = _cumsum_rows(tri, g_all)
    gcum_t = gcum.T
    zero = jnp.zeros((CHUNK, A_DK), F32)

    def diag2(x):
        return jnp.concatenate([jnp.concatenate([x[:CHUNK], zero], axis=1),
                                jnp.concatenate([zero, x[CHUNK:]], axis=1)], axis=0)


    r, k2, v, bv, c_last, p_tail, atrt, v_st, bgram = ([] for _ in range(9))
    for p, sl in enumerate(blanes):
        kk = kk_all[:, sl]
        kk = kk * lax.rsqrt(_head_sum(kk * kk, left) + 1e-6)
        cum = cum_all[:, sl]
        r.append(r_all[:, sl])
        k2.append(k2_all[:, sl])
        v.append(v_all[:, sl])
        bv.append(kk * a_all[:, sl])
        c_last.append(cum[CHUNK - 1:CHUNK, :])
        p_inv = jnp.exp(-cum)
        p_tail.append(jnp.exp(c_last[p] - cum))
        at = -kk * jnp.exp(cum - logw_all[:, sl])
        rt = r[p] * jnp.exp(cum)
        atrt.append(jnp.concatenate([at, rt], axis=0))
        v_st.append(_stack(v[p], left))
        bgram.append(_mm(jnp.concatenate([_stack(at, left), _stack(rt, left)], axis=0),
                         jnp.concatenate([_stack(bv[p] * p_inv, left), _stack(k2[p] * p_inv, left)], axis=0),
                         MODE_GRAM, NT))

    gc, g_last, decay, k_st, q_st, kb_st, vb_st, egc, agram = ([] for _ in range(9))
    for p in apairs:
        qs, ks, vs, betas = [], [], [], []
        for h in (2 * p, 2 * p + 1):
            lo = h * A_DK
            q = qkv[:, lo:lo + A_DK]
            k = qkv[:, D_MODEL + lo:D_MODEL + lo + A_DK]
            qs.append(q * lax.rsqrt(jnp.sum(q * q, axis=-1, keepdims=True) + 1e-6) * (A_DK ** -0.5))
            ks.append(k * lax.rsqrt(jnp.sum(k * k, axis=-1, keepdims=True) + 1e-6))
            vs.append(qkv[:, 2 * D_MODEL + lo:2 * D_MODEL + lo + A_DK])
            betas.append(beta_all[:, h:h + 1])
        la, lb = A_HEADS + 2 * p, A_HEADS + 2 * p + 1
        gc_p = jnp.concatenate([gcum[:, la:la + 1], gcum[:, lb:lb + 1]], axis=0)
        gr_p = jnp.concatenate([gcum_t[la:la + 1, :], gcum_t[lb:lb + 1, :]], axis=1)
        g_last.append(jnp.concatenate([jnp.broadcast_to(gc_p[CHUNK - 1:CHUNK, :], (CHUNK, 1)),
                                       jnp.broadcast_to(gc_p[PAIR - 1:PAIR, :], (CHUNK, 1))], axis=0))
        decay.append(jnp.where(incl, jnp.exp(jnp.where(incl, gc_p - gr_p, 0.0)), 0.0))
        beta = jnp.concatenate(betas, axis=0)
        k_st.append(jnp.concatenate(ks, axis=0))
        q_st.append(jnp.concatenate(qs, axis=0))
        kb_st.append(k_st[p] * beta)
        vb_st.append(jnp.concatenate(vs, axis=0) * beta)
        gc.append(gc_p)
        egc.append(jnp.exp(gc_p))
        agram.append(_mm(jnp.concatenate([diag2(kb_st[p]), diag2(q_st[p])], axis=0), diag2(k_st[p]),
                         MODE_GRAM, NT))

    x0 = [_mm(atrt[p], sb_ref[p], MODE_SREAD, NT) for p in bpairs]
    akv = [_mm(jnp.where(strict, bgram[p][:PAIR, PAIR:], 0.0), v_st[p], MODE_APPLY) for p in bpairs]
    attn = [agram[p][PAIR:] * decay[p] for p in apairs]

    tinv = _tri_inv([jnp.where(strict, agram[p][:PAIR] * decay[p], 0.0) for p in apairs]
                    + [jnp.where(strict, -bgram[p][:PAIR, :PAIR], 0.0) for p in bpairs], eye, blocks)
    atinv, btinv = tinv[:len(apairs)], tinv[len(apairs):]

    uw = [_mm(atinv[p], jnp.concatenate([vb_st[p], kb_st[p] * egc[p]], axis=1), MODE_APPLY)
          for p in apairs]
    u = [_fold(_mm(btinv[p], _stack(x0[p][:CHUNK] + _fold(akv[p]), left), MODE_APPLY)) for p in bpairs]
    ws_qs = [_mm(jnp.concatenate([diag2(uw[p][:, A_DK:]), diag2(q_st[p] * egc[p])], axis=0),
                 jnp.concatenate([sa_ref[2 * p], sa_ref[2 * p + 1]], axis=0), MODE_SREAD)
             for p in apairs]
    yy = [_mm(jnp.where(jnp.concatenate([incl, incl], axis=1), bgram[p][PAIR:, :], 0.0),
              jnp.concatenate([_stack(u[p], left), v_st[p]], axis=0), MODE_APPLY) for p in bpairs]
    bupd = [_mm(jnp.concatenate([u[p], v[p]], axis=0),
                jnp.concatenate([bv[p] * p_tail[p], k2[p] * p_tail[p]], axis=0), MODE_SUPD, TN)
            for p in bpairs]
    v_new = [uw[p][:, :A_DK] - ws_qs[p][:PAIR] for p in apairs]
    av = [_mm(attn[p], v_new[p], MODE_APPLY) for p in apairs]
    aupd = [_mm(k_st[p] * jnp.exp(g_last[p] - gc[p]), diag2(v_new[p]), MODE_SUPD, TN)
            for p in apairs]

    for p, sl in enumerate(blanes):
        sb_ref[p] = sb_ref[p] * jnp.exp(c_last[p]) + jnp.where(same, bupd[p], 0.0)
        y = x0[p][CHUNK:] + _fold(yy[p])
        mean = _head_sum(y, left) * (1.0 / B_N)
        yc = y - mean
        var = _head_sum(yc * yc, left) * (1.0 / B_N)
        yn = yc * lax.rsqrt(var + B_GN_EPS) * lng_ref[:, sl] + lnb_ref[:, sl]
        bonus = _head_sum(r[p] * k2[p] * rk_ref[:, sl], left) * v[p]
        ob_ref[:, sl] = (yn + bonus) * gate_all[:, sl]

    for p in apairs:
        e_last = jnp.exp(g_last[p])
        sa_ref[2 * p] = sa_ref[2 * p] * e_last[0:1, :] + aupd[p][:, :A_DK]
        sa_ref[2 * p + 1] = sa_ref[2 * p + 1] * e_last[CHUNK:CHUNK + 1, :] + aupd[p][:, A_DK:]
        o = ws_qs[p][PAIR:] + av[p]
        o = o * lax.rsqrt(jnp.mean(o * o, axis=-1, keepdims=True) + EPS) * again_ref[...]
        for i, h in enumerate((2 * p, 2 * p + 1)):
            lo = h * A_DK
            zg = za_ref[:, 3 * D_MODEL + lo:3 * D_MODEL + lo + A_DK]
            oa_ref[:, lo:lo + A_DK] = o[i * CHUNK:(i + 1) * CHUNK] * _silu(zg)


def _mixers(z3, convw, rate, dtb, again, mub, mus, w0, wup, a0, aup, gup, kk, ka, rk, lng, lnb):
    bsz, tp, _ = z3.shape
    nc = tp // CHUNK
    chunk = lambda w, j: pl.BlockSpec((None, CHUNK, w), lambda b, c: (b, (c + nc - 1) % nc, j))
    const = lambda r, w: pl.BlockSpec((r, w), lambda b, c: (0, 0))
    out = jax.ShapeDtypeStruct((bsz, tp, D_MODEL), F32)
    return pl.pallas_call(
        _mixer_kernel,
        grid=(bsz, nc),
        in_specs=[
            chunk(4 * D_MODEL, COL_A // (4 * D_MODEL)),
            chunk(3 * D_MODEL, COL_B // (3 * D_MODEL)),
            chunk(S_COLS, COL_S // S_COLS),
            const(A_CONV, 3 * D_MODEL), const(1, LANES), const(1, LANES), const(1, A_DK),
            const(1, 3 * D_MODEL), const(1, S_COLS - S_WA),
            const(1, D_MODEL), const(LANES, D_MODEL), const(1, D_MODEL), const(LANES, D_MODEL),
            const(2 * LANES, D_MODEL),
            const(1, D_MODEL), const(1, D_MODEL), const(1, D_MODEL), const(1, D_MODEL), const(1, D_MODEL),
        ],
        out_specs=[chunk(D_MODEL, 0), chunk(D_MODEL, 0)],
        out_shape=[out, out],
        scratch_shapes=[
            pltpu.VMEM((TAIL + CHUNK, 3 * D_MODEL), F32),
            pltpu.VMEM((TAIL + CHUNK, 3 * D_MODEL), F32),
            pltpu.VMEM((TAIL + CHUNK, S_COLS - S_WA), F32),
            pltpu.VMEM((A_HEADS, A_DK, A_DK), F32),
            pltpu.VMEM((B_HEADS // 2, PAIR, PAIR), F32),
        ],
        compiler_params=pltpu.CompilerParams(
            dimension_semantics=("parallel", "arbitrary"), vmem_limit_bytes=VMEM_LIMIT),
        name="mixers",
    )(z3, z3, z3, convw, rate, dtb, again, mub, mus, w0, wup, a0, aup, gup, kk, ka, rk, lng, lnb)


def _out_kernel(h_ref, g_ref, oa_ref, ob_ref, wout_ref, gain2_ref, wgu_ref, wd_ref, fgain_ref,
                o_ref, act_ref):
    merged = (_sigmoid(g_ref[:, 0:D_MODEL]) * oa_ref[...]
              + _sigmoid(g_ref[:, D_MODEL:2 * D_MODEL]) * ob_ref[...])
    h = h_ref[...] + jnp.dot(merged.astype(BF16), wout_ref[...], preferred_element_type=F32)
    xn = _rms(h, gain2_ref[...]).astype(BF16)
    h = h + 0.5 * _swiglu(xn, wgu_ref, wd_ref, act_ref)
    o_ref[...] = _rms(h, fgain_ref[...])


def _out_ffn2(h3, z3, oa, ob, wout, gain2, wgu, wd, fgain, seq, tm):
    bsz = h3.shape[0]
    const = dict(pipeline_mode=pl.Buffered(1))
    tile = lambda w, j: pl.BlockSpec((None, tm, w), lambda b, i: (b, i, j))
    return pl.pallas_call(
        _out_kernel,
        grid=(bsz, seq // tm),
        in_specs=[
            tile(D_MODEL, 0),
            tile(2 * D_MODEL, COL_G // (2 * D_MODEL)),
            tile(D_MODEL, 0),
            tile(D_MODEL, 0),
            pl.BlockSpec((D_MODEL, D_MODEL), lambda b, i: (0, 0), **const),
            pl.BlockSpec((1, D_MODEL), lambda b, i: (0, 0)),
            pl.BlockSpec((D_MODEL, 2 * D_FF), lambda b, i: (0, 0), **const),
            pl.BlockSpec((D_FF, D_MODEL), lambda b, i: (0, 0), **const),
            pl.BlockSpec((1, D_MODEL), lambda b, i: (0, 0)),
        ],
        out_specs=tile(D_MODEL, 0),
        out_shape=jax.ShapeDtypeStruct((bsz, seq, D_MODEL), F32),
        scratch_shapes=[pltpu.VMEM((tm, D_FF), BF16)],
        compiler_params=pltpu.CompilerParams(
            dimension_semantics=("parallel", "parallel"), vmem_limit_bytes=VMEM_LIMIT),
        name="out_ffn2",
    )(h3, z3, oa, ob, wout, gain2, wgu, wd, fgain)


def _regroup_in_weight(w):
    b0 = 4 * D_MODEL + 2 * A_HEADS
    lora0 = b0 + 3 * D_MODEL
    gate0 = lora0 + W_LORA + AA_LORA + G_LORA
    zeros = lambda n: jnp.zeros((w.shape[0], n), w.dtype)
    return jnp.concatenate([
        w[:, 0:4 * D_MODEL],
        w[:, gate0:gate0 + 2 * D_MODEL],
        w[:, b0:lora0],
        w[:, 4 * D_MODEL:b0], zeros(LANES - 2 * A_HEADS),
        w[:, lora0:lora0 + W_LORA + AA_LORA],
        w[:, lora0 + W_LORA + AA_LORA:gate0], zeros(2 * LANES - G_LORA),
    ], axis=1)


def kernel(x, meta_tokens, ffn1_norm, ffn1_w_gu, ffn1_w_down, mix_norm, w_in, a_conv_w, a_log_rate, a_dt_bias, a_out_norm, b_shift_mu, b_w0, b_w_up, b_a0, b_a_up, b_g_up, b_k_k, b_k_a, b_r_k, b_ln_gain, b_ln_bias, w_out, ffn2_norm, ffn2_w_gu, ffn2_w_down, final_norm):
    bsz, seq, _ = x.shape
    tp = seq + CHUNK
    row = lambda v: v.reshape(1, -1).astype(F32)

    tail = jnp.concatenate([jnp.zeros((PAD_ROWS, D_MODEL), x.dtype), meta_tokens.astype(x.dtype)], axis=0)
    h = jnp.concatenate([x, jnp.broadcast_to(tail[None], (bsz, CHUNK, D_MODEL))], axis=1)
    h = h.reshape(bsz * tp, D_MODEL)

    l = 0
    h1, u = _ffn1(h, row(ffn1_norm[l]), ffn1_w_gu[l].astype(BF16), ffn1_w_down[l].astype(BF16),
                  row(mix_norm[l]), tm=640)
    z = _inproj(u, _regroup_in_weight(w_in[l]).astype(BF16), tm=1280, tn=2432)
    z3 = z.reshape(bsz, tp, IN_COLS)

    lane_pad = lambda v: jnp.pad(v.astype(F32), (A_HEADS, LANES - 2 * A_HEADS)).reshape(1, LANES)
    mu = b_shift_mu[l].astype(F32)
    mu_b = mu[:3 * D_MODEL].reshape(1, -1)
    mu_s = jnp.pad(mu[3 * D_MODEL:], (0, 2 * LANES - G_LORA)).reshape(1, -1)
    wup = jnp.pad(b_w_up[l].astype(F32), ((0, AA_LORA), (0, 0)))
    aup = jnp.pad(b_a_up[l].astype(F32), ((W_LORA, 0), (0, 0)))
    gup = jnp.pad(b_g_up[l].astype(F32), ((0, 2 * LANES - G_LORA), (0, 0)))
    o_a, o_b = _mixers(
        z3, a_conv_w[l].astype(F32), lane_pad(jnp.exp(a_log_rate[l].astype(F32))), lane_pad(a_dt_bias[l]),
        row(a_out_norm[l]), mu_b, mu_s, row(b_w0[l]), wup, row(b_a0[l]), aup, gup,
        row(b_k_k[l]), row(b_k_a[l]), row(b_r_k[l]), row(b_ln_gain[l]), row(b_ln_bias[l]))

    return _out_ffn2(h1.reshape(bsz, tp, D_MODEL), z3, o_a, o_b, w_out[l].astype(BF16),
                     row(ffn2_norm[l]), ffn2_w_gu[l].astype(BF16), ffn2_w_down[l].astype(BF16),
                     row(final_norm), seq=seq, tm=512)
```

```python
import numpy as np

import jax
import jax.numpy as jnp
from jax import lax
from jax.experimental import pallas as pl
from jax.experimental.pallas import tpu as pltpu

F32 = jnp.float32
BF16 = jnp.bfloat16

D_MODEL = 1024
N_META = 16
EPS = 1e-6
D_FF = 2816
CHUNK = 64
A_DK = 128
A_HEADS = 8
A_CONV = 4
B_N = 64
B_HEADS = 16
W_LORA = 64
AA_LORA = 64
G_LORA = 160
B_GN_EPS = B_N * 1e-5

PAD_ROWS = CHUNK - N_META
FF_CHUNK = 256
TAIL = 8
PAIR = 2 * CHUNK
LANES = 128

COL_A = 0
COL_G = 4096
COL_B = 6144
COL_S = 9216
S_BA = 0
S_WA = 128
S_GD = 256
S_COLS = 512
IN_COLS = 9728

VMEM_LIMIT = 56 * 1024 * 1024

NN = (((1,), (0,)), ((), ()))
NT = (((1,), (1,)), ((), ()))
TN = (((0,), (0,)), ((), ()))

M_EYE, M_BLK8, M_OFF16, M_OFF32, M_OFF64, M_STRICT, M_NSTRICT, M_INCL = range(8)


def _mm(a, b, dims=NN):
    return lax.dot_general(a.astype(BF16), b.astype(BF16), dims, preferred_element_type=F32)


def _cumsum_rows(tri, x):
    hi = x.astype(BF16)
    lo = (x - hi.astype(F32)).astype(BF16)
    return jnp.dot(tri, hi, preferred_element_type=F32) + jnp.dot(tri, lo, preferred_element_type=F32)


def _sigmoid(x):
    return 1.0 / (1.0 + jnp.exp(-x))


def _silu(x):
    return x * _sigmoid(x)


def _softplus(x):
    return jnp.maximum(x, 0.0) + jnp.log(1.0 + jnp.exp(-jnp.abs(x)))


def _rms(x, gain):
    return x * lax.rsqrt(jnp.mean(x * x, axis=-1, keepdims=True) + EPS) * gain


def _pair_constants():
    row = np.arange(CHUNK)[:, None]
    col = np.arange(PAIR)[None, :] % CHUNK
    blk = lambda s: (row >> s) == (col >> s)
    strict = row > col
    planes = [row == col, blk(3), blk(4) & ~blk(3), blk(5) & ~blk(4), ~blk(5),
              strict, -1.0 * strict, row >= col]
    lanes = np.arange(LANES)[None, :] < B_N
    halves = [np.broadcast_to(lanes, (CHUNK, LANES)), np.broadcast_to(~lanes, (CHUNK, LANES))]
    return (jnp.asarray(np.stack([np.asarray(p, np.float32) for p in planes]), BF16),
            jnp.asarray(np.stack([np.asarray(h, np.float32) for h in halves]), BF16))


def _tri_inv(ms, mask, half):
    mm = lambda xs, ys: [_mm(x, _stack(y, half)).astype(BF16) for x, y in zip(xs, ys)]

    n1 = [m * mask(M_BLK8) for m in ms]
    n2 = mm(n1, n1)
    n4 = mm(n2, n2)
    d = mm([mask(M_EYE) - a for a in n1], [mask(M_EYE) + a for a in n2])
    d = mm(d, [mask(M_EYE) + a for a in n4])
    for level in (M_OFF16, M_OFF32, M_OFF64):
        de = mm(d, [m * mask(level) for m in ms])
        d = [a - b for a, b in zip(d, mm(de, d))]
    return d


def _stack(x, half):
    xb = x.astype(BF16)
    return jnp.concatenate([xb * half(0), xb * half(1)], axis=0)


def _head_sum(x, left):
    s0 = jnp.sum(jnp.where(left, x, 0.0), axis=-1, keepdims=True)
    s1 = jnp.sum(jnp.where(left, 0.0, x), axis=-1, keepdims=True)
    return jnp.where(left, s0, s1)


def _swiglu(xn, wgu_ref, wd_ref, act_ref):
    for c in range(D_FF // FF_CHUNK):
        lo = c * FF_CHUNK
        g = jnp.dot(xn, wgu_ref[:, lo:lo + FF_CHUNK], preferred_element_type=F32)
        u = jnp.dot(xn, wgu_ref[:, D_FF + lo:D_FF + lo + FF_CHUNK], preferred_element_type=F32)
        act_ref[:, lo:lo + FF_CHUNK] = (_silu(g) * u).astype(BF16)
    return jnp.dot(act_ref[...], wd_ref[...], preferred_element_type=F32)


def _ffn1_kernel(h_ref, gain_ref, wgu_ref, wd_ref, mixgain_ref, o_ref, u_ref, act_ref):
    h = h_ref[...]
    xn = _rms(h, gain_ref[...]).astype(BF16)
    h = h + 0.5 * _swiglu(xn, wgu_ref, wd_ref, act_ref)
    o_ref[...] = h
    u_ref[...] = _rms(h, mixgain_ref[...]).astype(BF16)


def _ffn1(h, gain, wgu, wd, mixgain, tm):
    n = h.shape[0]
    const = dict(pipeline_mode=pl.Buffered(1))
    tile = pl.BlockSpec((tm, D_MODEL), lambda i: (i, 0))
    vec = pl.BlockSpec((1, D_MODEL), lambda i: (0, 0))
    return pl.pallas_call(
        _ffn1_kernel,
        grid=(n // tm,),
        in_specs=[
            tile, vec,
            pl.BlockSpec((D_MODEL, 2 * D_FF), lambda i: (0, 0), **const),
            pl.BlockSpec((D_FF, D_MODEL), lambda i: (0, 0), **const),
            vec,
        ],
        out_specs=[tile, tile],
        out_shape=[jax.ShapeDtypeStruct((n, D_MODEL), F32), jax.ShapeDtypeStruct((n, D_MODEL), BF16)],
        scratch_shapes=[pltpu.VMEM((tm, D_FF), BF16)],
        compiler_params=pltpu.CompilerParams(
            dimension_semantics=("parallel",), vmem_limit_bytes=VMEM_LIMIT),
        name="ffn1",
    )(h, gain, wgu, wd, mixgain)


def _inproj_kernel(u_ref, w_ref, o_ref):
    o_ref[...] = jnp.dot(u_ref[...], w_ref[...], preferred_element_type=F32)


def _inproj(u, w, tm, tn):
    n = u.shape[0]
    return pl.pallas_call(
        _inproj_kernel,
        grid=(IN_COLS // tn, n // tm),
        in_specs=[
            pl.BlockSpec((tm, D_MODEL), lambda j, i: (i, 0)),
            pl.BlockSpec((D_MODEL, tn), lambda j, i: (0, j)),
        ],
        out_specs=pl.BlockSpec((tm, tn), lambda j, i: (i, j)),
        out_shape=jax.ShapeDtypeStruct((n, IN_COLS), F32),
        compiler_params=pltpu.CompilerParams(
            dimension_semantics=("parallel", "parallel"), vmem_limit_bytes=VMEM_LIMIT),
        name="in_proj",
    )(u, w)


ROWS = 2


def _mixer_kernel(za_ref, zb_ref, sm_ref, mask_ref, half_ref,
                  convw_ref, rate_ref, dtb_ref, again_ref,
                  mub_ref, mus_ref, w0_ref, wup_ref, a0_ref, aup_ref, gup_ref,
                  kk_ref, ka_ref, rk_ref, lng_ref, lnb_ref,
                  oa_ref, ob_ref,
                  exta_ref, extb_ref, exts_ref, sa_ref, sb_ref):
    c = pl.program_id(1)
    wqkv = 3 * D_MODEL
    ws = S_COLS - S_WA

    @pl.when(c == 0)
    def _():
        sa_ref[...] = jnp.zeros_like(sa_ref)
        sb_ref[...] = jnp.zeros_like(sb_ref)
        exta_ref[:, 0:TAIL, :] = jnp.zeros((ROWS, TAIL, wqkv), F32)
        extb_ref[:, 0:TAIL, :] = jnp.zeros((ROWS, TAIL, wqkv), F32)
        exts_ref[:, 0:TAIL, :] = jnp.zeros((ROWS, TAIL, ws), F32)

    mask = lambda i: mask_ref[i]
    half = lambda i: half_ref[i]
    left = lax.broadcasted_iota(jnp.int32, (CHUNK, LANES), 1) < B_N
    tri = lambda: mask_ref[M_INCL, 0:CHUNK, 0:CHUNK]
    incl2 = lambda: jnp.concatenate([mask(M_INCL), mask(M_INCL)], axis=1)
    apairs = range(A_HEADS // 2)
    bpairs = range(B_HEADS // 2)
    blanes = [slice(p * LANES, (p + 1) * LANES) for p in bpairs]
    prow = lax.broadcasted_iota(jnp.int32, (PAIR, PAIR), 0)
    pcol = lax.broadcasted_iota(jnp.int32, (PAIR, PAIR), 1)
    same = (prow >> 6) == (pcol >> 6)
    incl = (lax.broadcasted_iota(jnp.int32, (CHUNK, PAIR), 0)
            >= (lax.broadcasted_iota(jnp.int32, (CHUNK, PAIR), 1) & (CHUNK - 1)))
    lcat = lambda a, b: jnp.concatenate([a, b], axis=1)
    rcat = lambda a, b: jnp.concatenate([a, b], axis=0)

    def diag2(xa, xb):
        za, zb = jnp.zeros(xa.shape, BF16), jnp.zeros(xb.shape, BF16)
        return jnp.concatenate([jnp.concatenate([xa.astype(BF16), zb], axis=1),
                                jnp.concatenate([za, xb.astype(BF16)], axis=1)], axis=0)


    def prepare(bi, tie):
        hold = (lambda n: 0.0) if tie is None else (lambda n: jnp.tile(tie, (1, n // LANES)))
        o = {}
        curb = zb_ref[bi]
        extb_ref[bi, TAIL:TAIL + CHUNK, :] = curb
        prevb = extb_ref[bi, TAIL - 1:TAIL - 1 + CHUNK, :]
        extb_ref[bi, 0:TAIL, :] = curb[CHUNK - TAIL:, :]
        zb = curb + (prevb - curb) * (mub_ref[...] + hold(wqkv))
        curs = sm_ref[bi, :, S_WA:S_COLS]
        exts_ref[bi, TAIL:TAIL + CHUNK, :] = curs
        prevs = exts_ref[bi, TAIL - 1:TAIL - 1 + CHUNK, :]
        exts_ref[bi, 0:TAIL, :] = curs[CHUNK - TAIL:, :]
        zs = curs + (prevs - curs) * (mus_ref[...] + hold(ws))

        r_all = zb[:, 0:D_MODEL]
        k_all = zb[:, D_MODEL:2 * D_MODEL]
        v_all = zb[:, 2 * D_MODEL:3 * D_MODEL]
        wa = zs[:, 0:LANES]
        gd = zs[:, LANES:3 * LANES]

        w_log = -_softplus(-(w0_ref[...] + _mm(jnp.tanh(wa), wup_ref[...]))) - 0.5
        logw_all = -jnp.exp(w_log)
        a_all = _sigmoid(a0_ref[...] + _mm(wa, aup_ref[...]))
        o["gate"] = _mm(_sigmoid(gd), gup_ref[...])
        kk_all = k_all * kk_ref[...]
        k2_all = k_all * (1.0 + (a_all - 1.0) * ka_ref[...])
        cum_all = _cumsum_rows(tri(), logw_all)

        cura = za_ref[bi, :, 0:wqkv]
        exta_ref[bi, TAIL:TAIL + CHUNK, :] = cura
        conv = (convw_ref[0:1, :] + hold(wqkv)) * exta_ref[bi, TAIL - 3:TAIL - 3 + CHUNK, :]
        for j in range(1, A_CONV):
            lo = TAIL - (A_CONV - 1) + j
            conv = conv + convw_ref[j:j + 1, :] * exta_ref[bi, lo:lo + CHUNK, :]
        exta_ref[bi, 0:TAIL, :] = cura[CHUNK - TAIL:, :]
        qkv = _silu(conv)

        sm = sm_ref[bi, :, S_BA:S_BA + LANES]
        rows = lax.broadcasted_iota(jnp.int32, (CHUNK, LANES), 0)
        real = jnp.logical_or(c != 0, rows >= PAD_ROWS)
        beta_all = jnp.where(real, _sigmoid(sm), 0.0)
        g_all = jnp.where(real, -rate_ref[...] * _softplus(sm + (dtb_ref[...] + hold(LANES))), 0.0)
        gcum = _cumsum_rows(tri(), g_all)
        gcum_t = gcum.T

        for name in ("r", "k2", "v", "bv", "c_last", "p_tail", "atrt", "v_st", "bk_st"):
            o[name] = []
        for p, sl in enumerate(blanes):
            kk = kk_all[:, sl]
            kk = kk * lax.rsqrt(_head_sum(kk * kk, left) + 1e-6)
            cum = cum_all[:, sl]
            o["r"].append(r_all[:, sl])
            o["k2"].append(k2_all[:, sl])
            o["v"].append(v_all[:, sl])
            o["bv"].append(kk * a_all[:, sl])
            o["c_last"].append(cum[CHUNK - 1:CHUNK, :])
            p_inv = jnp.exp(-cum)
            o["p_tail"].append(jnp.exp(o["c_last"][p] - cum))
            at = (-kk * jnp.exp(cum - logw_all[:, sl])).astype(BF16)
            rt = (o["r"][p] * jnp.exp(cum)).astype(BF16)
            o["atrt"].append(jnp.concatenate([at, rt], axis=0))
            o["v_st"].append(_stack(o["v"][p], half))
            o["bk_st"].append(jnp.concatenate([_stack(o["bv"][p] * p_inv, half),
                                               _stack(o["k2"][p] * p_inv, half)], axis=0))

        for name in ("g_last", "decay", "vbe", "qe", "al", "ar", "ktail"):
            o[name] = []
        for p in apairs:
            qs, ks, kbs, gcs = [], [], [], []
            for h in (2 * p, 2 * p + 1):
                lo = h * A_DK
                q = qkv[:, lo:lo + A_DK]
                k = qkv[:, D_MODEL + lo:D_MODEL + lo + A_DK]
                beta = beta_all[:, h:h + 1]
                gc_h = gcum[:, A_HEADS + h:A_HEADS + h + 1]
                eg = jnp.exp(gc_h)
                gcs.append(gc_h)
                qs.append(q * lax.rsqrt(jnp.sum(q * q, axis=-1, keepdims=True) + 1e-6) * (A_DK ** -0.5))
                ks.append(k * lax.rsqrt(jnp.sum(k * k, axis=-1, keepdims=True) + 1e-6))
                kbs.append(ks[-1] * beta)
                o["vbe"].append(lcat(qkv[:, 2 * D_MODEL + lo:2 * D_MODEL + lo + A_DK] * beta, kbs[-1] * eg))
                o["qe"].append(qs[-1] * eg)
                o["g_last"].append(gc_h[CHUNK - 1:CHUNK, :])
                o["ktail"].append(ks[-1] * jnp.exp(o["g_last"][-1] - gc_h))
            la, lb = A_HEADS + 2 * p, A_HEADS + 2 * p + 1
            gr_p = lcat(gcum_t[la:la + 1, :], gcum_t[lb:lb + 1, :])
            gc_p = jnp.where(left, gcs[0], gcs[1])
            o["decay"].append(jnp.where(incl, jnp.exp(jnp.where(incl, gc_p - gr_p, 0.0)), 0.0))
            o["al"].append(rcat(lcat(kbs[0], kbs[1]), lcat(qs[0], qs[1])))
            o["ar"].append(diag2(ks[0], ks[1]))
        return o

    def grams(bi, o):
        o["bgram"] = [_mm(o["atrt"][p], o["bk_st"][p], NT).astype(BF16) for p in bpairs]
        o["agram"] = [_mm(o["al"][p], o["ar"][p], NT) for p in apairs]
        o["x0"] = [_mm(o["atrt"][p], sb_ref[bi, p], NT) for p in bpairs]
        o["akv"] = [_mm(o["bgram"][p][:CHUNK, PAIR:] * mask(M_STRICT), o["v_st"][p]) for p in bpairs]

    def recur(bi, o):
        bgram, agram, x0 = o["bgram"], o["agram"], o["x0"]
        attn = [agram[p][CHUNK:] * o["decay"][p] for p in apairs]
        tinv = _tri_inv([(agram[p][:CHUNK] * o["decay"][p]).astype(BF16) * mask(M_STRICT) for p in apairs]
                        + [bgram[p][:CHUNK, :PAIR] * mask(M_NSTRICT) for p in bpairs], mask, half)
        atinv, btinv = tinv[:len(apairs)], tinv[len(apairs):]
        uw = [_mm(atinv[p], diag2(o["vbe"][2 * p], o["vbe"][2 * p + 1])) for p in apairs]
        u = [_mm(btinv[p], _stack(x0[p][:CHUNK] + o["akv"][p], half)) for p in bpairs]
        ws_qs = [_mm(rcat(uw[h // 2][:, (2 * (h % 2) + 1) * A_DK:(2 * (h % 2) + 2) * A_DK], o["qe"][h]),
                     sa_ref[bi, h]) for h in range(A_HEADS)]
        o["yy"] = [_mm(bgram[p][CHUNK:, :] * incl2(),
                       jnp.concatenate([_stack(u[p], half), o["v_st"][p]], axis=0)) for p in bpairs]
        o["bupd"] = [_mm(jnp.concatenate([u[p], o["v"][p]], axis=0),
                         jnp.concatenate([o["bv"][p] * o["p_tail"][p], o["k2"][p] * o["p_tail"][p]], axis=0), TN)
                     for p in bpairs]
        v_new = [uw[h // 2][:, 2 * (h % 2) * A_DK:(2 * (h % 2) + 1) * A_DK] - ws_qs[h][:CHUNK]
                 for h in range(A_HEADS)]
        o["ws_qs"] = ws_qs
        o["av"] = [_mm(attn[p], diag2(v_new[2 * p], v_new[2 * p + 1])) for p in apairs]
        o["aupd"] = [_mm(o["ktail"][h], v_new[h], TN) for h in range(A_HEADS)]

    def finish(bi, o):
        for p, sl in enumerate(blanes):
            sb_ref[bi, p] = sb_ref[bi, p] * jnp.exp(o["c_last"][p]) + jnp.where(same, o["bupd"][p], 0.0)
            y = o["x0"][p][CHUNK:] + o["yy"][p]
            mean = _head_sum(y, left) * (1.0 / B_N)
            yc = y - mean
            var = _head_sum(yc * yc, left) * (1.0 / B_N)
            yn = yc * lax.rsqrt(var + B_GN_EPS) * lng_ref[:, sl] + lnb_ref[:, sl]
            bonus = _head_sum(o["r"][p] * o["k2"][p] * rk_ref[:, sl], left) * o["v"][p]
            ob_ref[bi, :, sl] = (yn + bonus) * o["gate"][:, sl]
        for h in range(A_HEADS):
            lo = h * A_DK
            sa_ref[bi, h] = sa_ref[bi, h] * jnp.exp(o["g_last"][h]) + o["aupd"][h]
            out = o["ws_qs"][h][CHUNK:] + o["av"][h // 2][:, (h % 2) * A_DK:(h % 2 + 1) * A_DK]
            out = out * lax.rsqrt(jnp.mean(out * out, axis=-1, keepdims=True) + EPS) * again_ref[...]
            zg = za_ref[bi, :, 3 * D_MODEL + lo:3 * D_MODEL + lo + A_DK]
            oa_ref[bi, :, lo:lo + A_DK] = out * _silu(zg)

    first = prepare(0, None)
    grams(0, first)
    anchor = first["bgram"][0][0:16, 0:LANES].astype(F32)[0:1]
    second = prepare(1, jnp.minimum(jnp.abs(anchor), 0.0))
    recur(0, first)
    grams(1, second)
    recur(1, second)
    finish(0, first)
    finish(1, second)


def _mixers(z3, convw, rate, dtb, again, mub, mus, w0, wup, a0, aup, gup, kk, ka, rk, lng, lnb):
    bsz, tp, _ = z3.shape
    nc = tp // CHUNK
    chunk = lambda w, j: pl.BlockSpec((ROWS, CHUNK, w), lambda b, c: (b, (c + nc - 1) % nc, j))
    const = lambda r, w: pl.BlockSpec((r, w), lambda b, c: (0, 0))
    out = jax.ShapeDtypeStruct((bsz, tp, D_MODEL), F32)
    masks, halves = _pair_constants()
    return pl.pallas_call(
        _mixer_kernel,
        grid=(bsz // ROWS, nc),
        in_specs=[
            chunk(4 * D_MODEL, COL_A // (4 * D_MODEL)),
            chunk(3 * D_MODEL, COL_B // (3 * D_MODEL)),
            chunk(S_COLS, COL_S // S_COLS),
            pl.BlockSpec(masks.shape, lambda b, c: (0, 0, 0)),
            pl.BlockSpec(halves.shape, lambda b, c: (0, 0, 0)),
            const(A_CONV, 3 * D_MODEL), const(1, LANES), const(1, LANES), const(1, A_DK),
            const(1, 3 * D_MODEL), const(1, S_COLS - S_WA),
            const(1, D_MODEL), const(LANES, D_MODEL), const(1, D_MODEL), const(LANES, D_MODEL),
            const(2 * LANES, D_MODEL),
            const(1, D_MODEL), const(1, D_MODEL), const(1, D_MODEL), const(1, D_MODEL), const(1, D_MODEL),
        ],
        out_specs=[chunk(D_MODEL, 0), chunk(D_MODEL, 0)],
        out_shape=[out, out],
        scratch_shapes=[
            pltpu.VMEM((ROWS, TAIL + CHUNK, 3 * D_MODEL), F32),
            pltpu.VMEM((ROWS, TAIL + CHUNK, 3 * D_MODEL), F32),
            pltpu.VMEM((ROWS, TAIL + CHUNK, S_COLS - S_WA), F32),
            pltpu.VMEM((ROWS, A_HEADS, A_DK, A_DK), F32),
            pltpu.VMEM((ROWS, B_HEADS // 2, PAIR, PAIR), F32),
        ],
        compiler_params=pltpu.CompilerParams(
            dimension_semantics=("parallel", "arbitrary"), vmem_limit_bytes=VMEM_LIMIT),
        name="mixers",
    )(z3, z3, z3, masks, halves, convw, rate, dtb, again, mub, mus, w0, wup, a0, aup, gup, kk, ka, rk, lng, lnb)


def _out_kernel(h_ref, g_ref, oa_ref, ob_ref, wout_ref, gain2_ref, wgu_ref, wd_ref, fgain_ref,
                o_ref, act_ref):
    merged = (_sigmoid(g_ref[:, 0:D_MODEL]) * oa_ref[...]
              + _sigmoid(g_ref[:, D_MODEL:2 * D_MODEL]) * ob_ref[...])
    h = h_ref[...] + jnp.dot(merged.astype(BF16), wout_ref[...], preferred_element_type=F32)
    xn = _rms(h, gain2_ref[...]).astype(BF16)
    h = h + 0.5 * _swiglu(xn, wgu_ref, wd_ref, act_ref)
    o_ref[...] = _rms(h, fgain_ref[...])


def _out_ffn2(h3, z3, oa, ob, wout, gain2, wgu, wd, fgain, seq, tm):
    bsz = h3.shape[0]
    const = dict(pipeline_mode=pl.Buffered(1))
    tile = lambda w, j: pl.BlockSpec((None, tm, w), lambda b, i: (b, i, j))
    return pl.pallas_call(
        _out_kernel,
        grid=(bsz, seq // tm),
        in_specs=[
            tile(D_MODEL, 0),
            tile(2 * D_MODEL, COL_G // (2 * D_MODEL)),
            tile(D_MODEL, 0),
            tile(D_MODEL, 0),
            pl.BlockSpec((D_MODEL, D_MODEL), lambda b, i: (0, 0), **const),
            pl.BlockSpec((1, D_MODEL), lambda b, i: (0, 0)),
            pl.BlockSpec((D_MODEL, 2 * D_FF), lambda b, i: (0, 0), **const),
            pl.BlockSpec((D_FF, D_MODEL), lambda b, i: (0, 0), **const),
            pl.BlockSpec((1, D_MODEL), lambda b, i: (0, 0)),
        ],
        out_specs=tile(D_MODEL, 0),
        out_shape=jax.ShapeDtypeStruct((bsz, seq, D_MODEL), F32),
        scratch_shapes=[pltpu.VMEM((tm, D_FF), BF16)],
        compiler_params=pltpu.CompilerParams(
            dimension_semantics=("parallel", "parallel"), vmem_limit_bytes=VMEM_LIMIT),
        name="out_ffn2",
    )(h3, z3, oa, ob, wout, gain2, wgu, wd, fgain)


def _regroup_in_weight(w):
    b0 = 4 * D_MODEL + 2 * A_HEADS
    lora0 = b0 + 3 * D_MODEL
    gate0 = lora0 + W_LORA + AA_LORA + G_LORA
    zeros = lambda n: jnp.zeros((w.shape[0], n), w.dtype)
    return jnp.concatenate([
        w[:, 0:4 * D_MODEL],
        w[:, gate0:gate0 + 2 * D_MODEL],
        w[:, b0:lora0],
        w[:, 4 * D_MODEL:b0], zeros(LANES - 2 * A_HEADS),
        w[:, lora0:lora0 + W_LORA + AA_LORA],
        w[:, lora0 + W_LORA + AA_LORA:gate0], zeros(2 * LANES - G_LORA),
    ], axis=1)


def kernel(x, meta_tokens, ffn1_norm, ffn1_w_gu, ffn1_w_down, mix_norm, w_in, a_conv_w, a_log_rate, a_dt_bias, a_out_norm, b_shift_mu, b_w0, b_w_up, b_a0, b_a_up, b_g_up, b_k_k, b_k_a, b_r_k, b_ln_gain, b_ln_bias, w_out, ffn2_norm, ffn2_w_gu, ffn2_w_down, final_norm):
    bsz, seq, _ = x.shape
    tp = seq + CHUNK
    row = lambda v: v.reshape(1, -1).astype(F32)

    tail = jnp.concatenate([jnp.zeros((PAD_ROWS, D_MODEL), x.dtype), meta_tokens.astype(x.dtype)], axis=0)
    h = jnp.concatenate([x, jnp.broadcast_to(tail[None], (bsz, CHUNK, D_MODEL))], axis=1)
    h = h.reshape(bsz * tp, D_MODEL)

    l = 0
    h1, u = _ffn1(h, row(ffn1_norm[l]), ffn1_w_gu[l].astype(BF16), ffn1_w_down[l].astype(BF16),
                  row(mix_norm[l]), tm=640)
    z = _inproj(u, _regroup_in_weight(w_in[l]).astype(BF16), tm=1280, tn=2432)
    z3 = z.reshape(bsz, tp, IN_COLS)

    lane_pad = lambda v: jnp.pad(v.astype(F32), (A_HEADS, LANES - 2 * A_HEADS)).reshape(1, LANES)
    mu = b_shift_mu[l].astype(F32)
    mu_b = mu[:3 * D_MODEL].reshape(1, -1)
    mu_s = jnp.pad(mu[3 * D_MODEL:], (0, 2 * LANES - G_LORA)).reshape(1, -1)
    wup = jnp.pad(b_w_up[l].astype(F32), ((0, AA_LORA), (0, 0)))
    aup = jnp.pad(b_a_up[l].astype(F32), ((W_LORA, 0), (0, 0)))
    gup = jnp.pad(b_g_up[l].astype(F32), ((0, 2 * LANES - G_LORA), (0, 0)))
    o_a, o_b = _mixers(
        z3, a_conv_w[l].astype(F32), lane_pad(jnp.exp(a_log_rate[l].astype(F32))), lane_pad(a_dt_bias[l]),
        row(a_out_norm[l]), mu_b, mu_s, row(b_w0[l]), wup, row(b_a0[l]), aup, gup,
        row(b_k_k[l]), row(b_k_a[l]), row(b_r_k[l]), row(b_ln_gain[l]), row(b_ln_bias[l]))

    return _out_ffn2(h1.reshape(bsz, tp, D_MODEL), z3, o_a, o_b, w_out[l].astype(BF16),
                     row(ffn2_norm[l]), ffn2_w_gu[l].astype(BF16), ffn2_w_down[l].astype(BF16),
                     row(final_norm), seq=seq, tm=512)
```

```python
import numpy as np

import jax
import jax.numpy as jnp
from jax import lax
from jax.experimental import pallas as pl
from jax.experimental.pallas import tpu as pltpu

F32 = jnp.float32
BF16 = jnp.bfloat16

D_MODEL = 1024
N_META = 16
EPS = 1e-6
D_FF = 2816
CHUNK = 64
A_DK = 128
A_HEADS = 8
A_CONV = 4
B_N = 64
B_HEADS = 16
W_LORA = 64
AA_LORA = 64
G_LORA = 160
B_GN_EPS = B_N * 1e-5

PAD_ROWS = CHUNK - N_META
FF_CHUNK = 256
TAIL = 8
PAIR = 2 * CHUNK
LANES = 128

COL_A = 0
COL_G = 4096
COL_B = 6144
COL_S = 9216
S_BA = 0
S_WA = 128
S_GD = 256
S_COLS = 512
IN_COLS = 9728

VMEM_LIMIT = 56 * 1024 * 1024

NN = (((1,), (0,)), ((), ()))
NT = (((1,), (1,)), ((), ()))
TN = (((0,), (0,)), ((), ()))

M_EYE, M_BLK8, M_OFF16, M_OFF32, M_OFF64, M_STRICT, M_NSTRICT, M_INCL = range(8)


def _mm(a, b, dims=NN):
    return lax.dot_general(a.astype(BF16), b.astype(BF16), dims, preferred_element_type=F32)


def _cumsum_rows(tri, x):
    hi = x.astype(BF16)
    lo = (x - hi.astype(F32)).astype(BF16)
    return jnp.dot(tri, hi, preferred_element_type=F32) + jnp.dot(tri, lo, preferred_element_type=F32)


def _sigmoid(x):
    return 1.0 / (1.0 + jnp.exp(-x))


def _silu(x):
    return x * _sigmoid(x)


def _softplus(x):
    return jnp.maximum(x, 0.0) + jnp.log(1.0 + jnp.exp(-jnp.abs(x)))


def _rms(x, gain):
    return x * lax.rsqrt(jnp.mean(x * x, axis=-1, keepdims=True) + EPS) * gain


def _pair_constants():
    row = np.arange(CHUNK)[:, None]
    col = np.arange(PAIR)[None, :] % CHUNK
    blk = lambda s: (row >> s) == (col >> s)
    strict = row > col
    planes = [row == col, blk(3), blk(4) & ~blk(3), blk(5) & ~blk(4), ~blk(5),
              strict, -1.0 * strict, row >= col]
    lanes = np.arange(LANES)[None, :] < B_N
    halves = [np.broadcast_to(lanes, (CHUNK, LANES)), np.broadcast_to(~lanes, (CHUNK, LANES))]
    return (jnp.asarray(np.stack([np.asarray(p, np.float32) for p in planes]), BF16),
            jnp.asarray(np.stack([np.asarray(h, np.float32) for h in halves]), BF16))


def _tri_inv(ms, mask, half):
    mm = lambda xs, ys: [_mm(x, _stack(y, half)).astype(BF16) for x, y in zip(xs, ys)]

    n1 = [m * mask(M_BLK8) for m in ms]
    n2 = mm(n1, n1)
    n4 = mm(n2, n2)
    d = mm([mask(M_EYE) - a for a in n1], [mask(M_EYE) + a for a in n2])
    d = mm(d, [mask(M_EYE) + a for a in n4])
    for level in (M_OFF16, M_OFF32, M_OFF64):
        de = mm(d, [m * mask(level) for m in ms])
        d = [a - b for a, b in zip(d, mm(de, d))]
    return d


def _stack(x, half):
    xb = x.astype(BF16)
    return jnp.concatenate([xb * half(0), xb * half(1)], axis=0)


def _head_sum(x, left):
    s0 = jnp.sum(jnp.where(left, x, 0.0), axis=-1, keepdims=True)
    s1 = jnp.sum(jnp.where(left, 0.0, x), axis=-1, keepdims=True)
    return jnp.where(left, s0, s1)


def _swiglu(xn, wgu_ref, wd_ref, act_ref):
    for c in range(D_FF // FF_CHUNK):
        lo = c * FF_CHUNK
        g = jnp.dot(xn, wgu_ref[:, lo:lo + FF_CHUNK], preferred_element_type=F32)
        u = jnp.dot(xn, wgu_ref[:, D_FF + lo:D_FF + lo + FF_CHUNK], preferred_element_type=F32)
        act_ref[:, lo:lo + FF_CHUNK] = (_silu(g) * u).astype(BF16)
    return jnp.dot(act_ref[...], wd_ref[...], preferred_element_type=F32)


def _ffn1_rows(h, gain_ref, wgu_ref, wd_ref, mixgain_ref, act_ref):
    xn = _rms(h, gain_ref[...]).astype(BF16)
    h = h + 0.5 * _swiglu(xn, wgu_ref, wd_ref, act_ref)
    return h, _rms(h, mixgain_ref[...]).astype(BF16)


def _ffn1_kernel(h_ref, gain_ref, wgu_ref, wd_ref, mixgain_ref, o_ref, u_ref, act_ref):
    o_ref[...], u_ref[...] = _ffn1_rows(h_ref[...], gain_ref, wgu_ref, wd_ref, mixgain_ref, act_ref)


def _ffn1_tail_kernel(t_ref, gain_ref, wgu_ref, wd_ref, mixgain_ref, h1_hbm, u_hbm, o_ref, u_ref, act_ref):
    del h1_hbm, u_hbm
    h1, u = _ffn1_rows(t_ref[...], gain_ref, wgu_ref, wd_ref, mixgain_ref, act_ref)
    o_ref[...] = jnp.broadcast_to(h1[None], o_ref.shape)
    u_ref[...] = jnp.broadcast_to(u[None], u_ref.shape)


def _ffn1_weight_specs(index):
    const = dict(pipeline_mode=pl.Buffered(1))
    vec = pl.BlockSpec((1, D_MODEL), index)
    return [vec,
            pl.BlockSpec((D_MODEL, 2 * D_FF), index, **const),
            pl.BlockSpec((D_FF, D_MODEL), index, **const),
            vec]


def _ffn1(x, gain, wgu, wd, mixgain, tm):
    bsz, seq, _ = x.shape
    tile = pl.BlockSpec((None, tm, D_MODEL), lambda b, i: (b, i, 0))
    return pl.pallas_call(
        _ffn1_kernel,
        grid=(bsz, seq // tm),
        in_specs=[tile] + _ffn1_weight_specs(lambda b, i: (0, 0)),
        out_specs=[tile, tile],
        out_shape=[jax.ShapeDtypeStruct((bsz, seq + CHUNK, D_MODEL), F32),
                   jax.ShapeDtypeStruct((bsz, seq + CHUNK, D_MODEL), BF16)],
        scratch_shapes=[pltpu.VMEM((tm, D_FF), BF16)],
        compiler_params=pltpu.CompilerParams(
            dimension_semantics=("parallel", "parallel"), vmem_limit_bytes=VMEM_LIMIT),
        name="ffn1",
    )(x, gain, wgu, wd, mixgain)


def _ffn1_tail(tail, h1, u, gain, wgu, wd, mixgain):
    bsz, tp, _ = h1.shape
    last = pl.BlockSpec((bsz, CHUNK, D_MODEL), lambda i: (0, tp // CHUNK - 1, 0))
    return pl.pallas_call(
        _ffn1_tail_kernel,
        grid=(1,),
        in_specs=([pl.BlockSpec((CHUNK, D_MODEL), lambda i: (0, 0))] + _ffn1_weight_specs(lambda i: (0, 0))
                  + [pl.BlockSpec(memory_space=pl.ANY), pl.BlockSpec(memory_space=pl.ANY)]),
        out_specs=[last, last],
        out_shape=[jax.ShapeDtypeStruct(h1.shape, h1.dtype), jax.ShapeDtypeStruct(u.shape, u.dtype)],
        input_output_aliases={5: 0, 6: 1},
        scratch_shapes=[pltpu.VMEM((CHUNK, D_FF), BF16)],
        compiler_params=pltpu.CompilerParams(
            dimension_semantics=("arbitrary",), vmem_limit_bytes=VMEM_LIMIT),
        name="ffn1_tail",
    )(tail, gain, wgu, wd, mixgain, h1, u)


def _inproj_kernel(u_ref, w_ref, o_ref):
    o_ref[...] = jnp.dot(u_ref[...], w_ref[...], preferred_element_type=F32)


def _inproj(u, w, tm, tn):
    n = u.shape[0]
    return pl.pallas_call(
        _inproj_kernel,
        grid=(IN_COLS // tn, n // tm),
        in_specs=[
            pl.BlockSpec((tm, D_MODEL), lambda j, i: (i, 0)),
            pl.BlockSpec((D_MODEL, tn), lambda j, i: (0, j)),
        ],
        out_specs=pl.BlockSpec((tm, tn), lambda j, i: (i, j)),
        out_shape=jax.ShapeDtypeStruct((n, IN_COLS), F32),
        compiler_params=pltpu.CompilerParams(
            dimension_semantics=("parallel", "parallel"), vmem_limit_bytes=VMEM_LIMIT),
        name="in_proj",
    )(u, w)


ROWS = 2


def _mixer_kernel(za_ref, zb_ref, sm_ref, mask_ref, half_ref,
                  convw_ref, rate_ref, dtb_ref, again_ref,
                  mub_ref, mus_ref, w0_ref, wup_ref, a0_ref, aup_ref, gup_ref,
                  kk_ref, ka_ref, rk_ref, lng_ref, lnb_ref,
                  oa_ref, ob_ref,
                  exta_ref, extb_ref, exts_ref, sa_ref, sb_ref):
    c = pl.program_id(1)
    wqkv = 3 * D_MODEL
    ws = S_COLS - S_WA

    @pl.when(c == 0)
    def _():
        sa_ref[...] = jnp.zeros_like(sa_ref)
        sb_ref[...] = jnp.zeros_like(sb_ref)
        exta_ref[:, 0:TAIL, :] = jnp.zeros((ROWS, TAIL, wqkv), F32)
        extb_ref[:, 0:TAIL, :] = jnp.zeros((ROWS, TAIL, wqkv), F32)
        exts_ref[:, 0:TAIL, :] = jnp.zeros((ROWS, TAIL, ws), F32)

    mask = lambda i: mask_ref[i]
    half = lambda i: half_ref[i]
    left = lax.broadcasted_iota(jnp.int32, (CHUNK, LANES), 1) < B_N
    tri = lambda: mask_ref[M_INCL, 0:CHUNK, 0:CHUNK]
    incl2 = lambda: jnp.concatenate([mask(M_INCL), mask(M_INCL)], axis=1)
    apairs = range(A_HEADS // 2)
    bpairs = range(B_HEADS // 2)
    blanes = [slice(p * LANES, (p + 1) * LANES) for p in bpairs]
    prow = lax.broadcasted_iota(jnp.int32, (PAIR, PAIR), 0)
    pcol = lax.broadcasted_iota(jnp.int32, (PAIR, PAIR), 1)
    same = (prow >> 6) == (pcol >> 6)
    incl = (lax.broadcasted_iota(jnp.int32, (CHUNK, PAIR), 0)
            >= (lax.broadcasted_iota(jnp.int32, (CHUNK, PAIR), 1) & (CHUNK - 1)))
    lcat = lambda a, b: jnp.concatenate([a, b], axis=1)
    rcat = lambda a, b: jnp.concatenate([a, b], axis=0)

    def diag2(xa, xb):
        za, zb = jnp.zeros(xa.shape, BF16), jnp.zeros(xb.shape, BF16)
        return jnp.concatenate([jnp.concatenate([xa.astype(BF16), zb], axis=1),
                                jnp.concatenate([za, xb.astype(BF16)], axis=1)], axis=0)


    def prepare(bi, tie):
        hold = (lambda n: 0.0) if tie is None else (lambda n: jnp.tile(tie, (1, n // LANES)))
        o = {}
        curb = zb_ref[bi]
        extb_ref[bi, TAIL:TAIL + CHUNK, :] = curb
        prevb = extb_ref[bi, TAIL - 1:TAIL - 1 + CHUNK, :]
        extb_ref[bi, 0:TAIL, :] = curb[CHUNK - TAIL:, :]
        zb = curb + (prevb - curb) * (mub_ref[...] + hold(wqkv))
        curs = sm_ref[bi, :, S_WA:S_COLS]
        exts_ref[bi, TAIL:TAIL + CHUNK, :] = curs
        prevs = exts_ref[bi, TAIL - 1:TAIL - 1 + CHUNK, :]
        exts_ref[bi, 0:TAIL, :] = curs[CHUNK - TAIL:, :]
        zs = curs + (prevs - curs) * (mus_ref[...] + hold(ws))

        r_all = zb[:, 0:D_MODEL]
        k_all = zb[:, D_MODEL:2 * D_MODEL]
        v_all = zb[:, 2 * D_MODEL:3 * D_MODEL]
        wa = zs[:, 0:LANES]
        gd = zs[:, LANES:3 * LANES]

        w_log = -_softplus(-(w0_ref[...] + _mm(jnp.tanh(wa), wup_ref[...]))) - 0.5
        logw_all = -jnp.exp(w_log)
        a_all = _sigmoid(a0_ref[...] + _mm(wa, aup_ref[...]))
        o["gate"] = _mm(_sigmoid(gd), gup_ref[...])
        kk_all = k_all * kk_ref[...]
        k2_all = k_all * (1.0 + (a_all - 1.0) * ka_ref[...])
        cum_all = _cumsum_rows(tri(), logw_all)

        cura = za_ref[bi, :, 0:wqkv]
        exta_ref[bi, TAIL:TAIL + CHUNK, :] = cura
        conv = (convw_ref[0:1, :] + hold(wqkv)) * exta_ref[bi, TAIL - 3:TAIL - 3 + CHUNK, :]
        for j in range(1, A_CONV):
            lo = TAIL - (A_CONV - 1) + j
            conv = conv + convw_ref[j:j + 1, :] * exta_ref[bi, lo:lo + CHUNK, :]
        exta_ref[bi, 0:TAIL, :] = cura[CHUNK - TAIL:, :]
        qkv = _silu(conv)

        sm = sm_ref[bi, :, S_BA:S_BA + LANES]
        rows = lax.broadcasted_iota(jnp.int32, (CHUNK, LANES), 0)
        real = jnp.logical_or(c != 0, rows >= PAD_ROWS)
        beta_all = jnp.where(real, _sigmoid(sm), 0.0)
        g_all = jnp.where(real, -rate_ref[...] * _softplus(sm + (dtb_ref[...] + hold(LANES))), 0.0)
        gcum = _cumsum_rows(tri(), g_all)
        gcum_t = gcum.T

        for name in ("r", "k2", "v", "bv", "c_last", "p_tail", "atrt", "v_st", "bk_st"):
            o[name] = []
        for p, sl in enumerate(blanes):
            kk = kk_all[:, sl]
            kk = kk * lax.rsqrt(_head_sum(kk * kk, left) + 1e-6)
            cum = cum_all[:, sl]
            o["r"].append(r_all[:, sl])
            o["k2"].append(k2_all[:, sl])
            o["v"].append(v_all[:, sl])
            o["bv"].append(kk * a_all[:, sl])
            o["c_last"].append(cum[CHUNK - 1:CHUNK, :])
            p_inv = jnp.exp(-cum)
            o["p_tail"].append(jnp.exp(o["c_last"][p] - cum))
            at = (-kk * jnp.exp(cum - logw_all[:, sl])).astype(BF16)
            rt = (o["r"][p] * jnp.exp(cum)).astype(BF16)
            o["atrt"].append(jnp.concatenate([at, rt], axis=0))
            o["v_st"].append(_stack(o["v"][p], half))
            o["bk_st"].append(jnp.concatenate([_stack(o["bv"][p] * p_inv, half),
                                               _stack(o["k2"][p] * p_inv, half)], axis=0))

        for name in ("g_last", "decay", "vbe", "qe", "al", "ar", "ktail"):
            o[name] = []
        for p in apairs:
            qs, ks, kbs, gcs = [], [], [], []
            for h in (2 * p, 2 * p + 1):
                lo = h * A_DK
                q = qkv[:, lo:lo + A_DK]
                k = qkv[:, D_MODEL + lo:D_MODEL + lo + A_DK]
                beta = beta_all[:, h:h + 1]
                gc_h = gcum[:, A_HEADS + h:A_HEADS + h + 1]
                eg = jnp.exp(gc_h)
                gcs.append(gc_h)
                qs.append(q * lax.rsqrt(jnp.sum(q * q, axis=-1, keepdims=True) + 1e-6) * (A_DK ** -0.5))
                ks.append(k * lax.rsqrt(jnp.sum(k * k, axis=-1, keepdims=True) + 1e-6))
                kbs.append(ks[-1] * beta)
                o["vbe"].append(lcat(qkv[:, 2 * D_MODEL + lo:2 * D_MODEL + lo + A_DK] * beta, kbs[-1] * eg))
                o["qe"].append(qs[-1] * eg)
                o["g_last"].append(gc_h[CHUNK - 1:CHUNK, :])
                o["ktail"].append(ks[-1] * jnp.exp(o["g_last"][-1] - gc_h))
            la, lb = A_HEADS + 2 * p, A_HEADS + 2 * p + 1
            gr_p = lcat(gcum_t[la:la + 1, :], gcum_t[lb:lb + 1, :])
            gc_p = jnp.where(left, gcs[0], gcs[1])
            o["decay"].append(jnp.where(incl, jnp.exp(jnp.where(incl, gc_p - gr_p, 0.0)), 0.0))
            o["al"].append(rcat(lcat(kbs[0], kbs[1]), lcat(qs[0], qs[1])))
            o["ar"].append(diag2(ks[0], ks[1]))
        return o

    def grams(bi, o):
        o["bgram"] = [_mm(o["atrt"][p], o["bk_st"][p], NT).astype(BF16) for p in bpairs]
        o["agram"] = [_mm(o["al"][p], o["ar"][p], NT) for p in apairs]
        o["x0"] = [_mm(o["atrt"][p], sb_ref[bi, p], NT) for p in bpairs]
        o["akv"] = [_mm(o["bgram"][p][:CHUNK, PAIR:] * mask(M_STRICT), o["v_st"][p]) for p in bpairs]

    def recur(bi, o):
        bgram, agram, x0 = o["bgram"], o["agram"], o["x0"]
        attn = [agram[p][CHUNK:] * o["decay"][p] for p in apairs]
        tinv = _tri_inv([(agram[p][:CHUNK] * o["decay"][p]).astype(BF16) * mask(M_STRICT) for p in apairs]
                        + [bgram[p][:CHUNK, :PAIR] * mask(M_NSTRICT) for p in bpairs], mask, half)
        atinv, btinv = tinv[:len(apairs)], tinv[len(apairs):]
        uw = [_mm(atinv[p], diag2(o["vbe"][2 * p], o["vbe"][2 * p + 1])) for p in apairs]
        u = [_mm(btinv[p], _stack(x0[p][:CHUNK] + o["akv"][p], half)) for p in bpairs]
        ws_qs = [_mm(rcat(uw[h // 2][:, (2 * (h % 2) + 1) * A_DK:(2 * (h % 2) + 2) * A_DK], o["qe"][h]),
                     sa_ref[bi, h]) for h in range(A_HEADS)]
        o["yy"] = [_mm(bgram[p][CHUNK:, :] * incl2(),
                       jnp.concatenate([_stack(u[p], half), o["v_st"][p]], axis=0)) for p in bpairs]
        o["bupd"] = [_mm(jnp.concatenate([u[p], o["v"][p]], axis=0),
                         jnp.concatenate([o["bv"][p] * o["p_tail"][p], o["k2"][p] * o["p_tail"][p]], axis=0), TN)
                     for p in bpairs]
        v_new = [uw[h // 2][:, 2 * (h % 2) * A_DK:(2 * (h % 2) + 1) * A_DK] - ws_qs[h][:CHUNK]
                 for h in range(A_HEADS)]
        o["ws_qs"] = ws_qs
        o["av"] = [_mm(attn[p], diag2(v_new[2 * p], v_new[2 * p + 1])) for p in apairs]
        o["aupd"] = [_mm(o["ktail"][h], v_new[h], TN) for h in range(A_HEADS)]

    def finish(bi, o):
        for p, sl in enumerate(blanes):
            sb_ref[bi, p] = sb_ref[bi, p] * jnp.exp(o["c_last"][p]) + jnp.where(same, o["bupd"][p], 0.0)
            y = o["x0"][p][CHUNK:] + o["yy"][p]
            mean = _head_sum(y, left) * (1.0 / B_N)
            yc = y - mean
            var = _head_sum(yc * yc, left) * (1.0 / B_N)
            yn = yc * lax.rsqrt(var + B_GN_EPS) * lng_ref[:, sl] + lnb_ref[:, sl]
            bonus = _head_sum(o["r"][p] * o["k2"][p] * rk_ref[:, sl], left) * o["v"][p]
            ob_ref[bi, :, sl] = (yn + bonus) * o["gate"][:, sl]
        for h in range(A_HEADS):
            lo = h * A_DK
            sa_ref[bi, h] = sa_ref[bi, h] * jnp.exp(o["g_last"][h]) + o["aupd"][h]
            out = o["ws_qs"][h][CHUNK:] + o["av"][h // 2][:, (h % 2) * A_DK:(h % 2 + 1) * A_DK]
            out = out * lax.rsqrt(jnp.mean(out * out, axis=-1, keepdims=True) + EPS) * again_ref[...]
            zg = za_ref[bi, :, 3 * D_MODEL + lo:3 * D_MODEL + lo + A_DK]
            oa_ref[bi, :, lo:lo + A_DK] = out * _silu(zg)

    ops = [prepare(0, None)]
    grams(0, ops[0])
    for bi in range(1, ROWS):
        anchor = ops[bi - 1]["bgram"][0][0:16, 0:LANES].astype(F32)[0:1]
        ops.append(prepare(bi, jnp.minimum(jnp.abs(anchor), 0.0)))
        recur(bi - 1, ops[bi - 1])
        grams(bi, ops[bi])
        finish(bi - 1, ops[bi - 1])
    recur(ROWS - 1, ops[ROWS - 1])
    finish(ROWS - 1, ops[ROWS - 1])


def _mixers(z3, convw, rate, dtb, again, mub, mus, w0, wup, a0, aup, gup, kk, ka, rk, lng, lnb):
    bsz, tp, _ = z3.shape
    nc = tp // CHUNK
    chunk = lambda w, j: pl.BlockSpec((ROWS, CHUNK, w), lambda b, c: (b, (c + nc - 1) % nc, j))
    const = lambda r, w: pl.BlockSpec((r, w), lambda b, c: (0, 0))
    out = jax.ShapeDtypeStruct((bsz, tp, D_MODEL), F32)
    masks, halves = _pair_constants()
    return pl.pallas_call(
        _mixer_kernel,
        grid=(bsz // ROWS, nc),
        in_specs=[
            chunk(4 * D_MODEL, COL_A // (4 * D_MODEL)),
            chunk(3 * D_MODEL, COL_B // (3 * D_MODEL)),
            chunk(S_COLS, COL_S // S_COLS),
            pl.BlockSpec(masks.shape, lambda b, c: (0, 0, 0)),
            pl.BlockSpec(halves.shape, lambda b, c: (0, 0, 0)),
            const(A_CONV, 3 * D_MODEL), const(1, LANES), const(1, LANES), const(1, A_DK),
            const(1, 3 * D_MODEL), const(1, S_COLS - S_WA),
            const(1, D_MODEL), const(LANES, D_MODEL), const(1, D_MODEL), const(LANES, D_MODEL),
            const(2 * LANES, D_MODEL),
            const(1, D_MODEL), const(1, D_MODEL), const(1, D_MODEL), const(1, D_MODEL), const(1, D_MODEL),
        ],
        out_specs=[chunk(D_MODEL, 0), chunk(D_MODEL, 0)],
        out_shape=[out, out],
        scratch_shapes=[
            pltpu.VMEM((ROWS, TAIL + CHUNK, 3 * D_MODEL), F32),
            pltpu.VMEM((ROWS, TAIL + CHUNK, 3 * D_MODEL), F32),
            pltpu.VMEM((ROWS, TAIL + CHUNK, S_COLS - S_WA), F32),
            pltpu.VMEM((ROWS, A_HEADS, A_DK, A_DK), F32),
            pltpu.VMEM((ROWS, B_HEADS // 2, PAIR, PAIR), F32),
        ],
        compiler_params=pltpu.CompilerParams(
            dimension_semantics=("parallel", "arbitrary"), vmem_limit_bytes=VMEM_LIMIT),
        name="mixers",
    )(z3, z3, z3, masks, halves, convw, rate, dtb, again, mub, mus, w0, wup, a0, aup, gup, kk, ka, rk, lng, lnb)


def _out_kernel(h_ref, g_ref, oa_ref, ob_ref, wout_ref, gain2_ref, wgu_ref, wd_ref, fgain_ref,
                o_ref, act_ref):
    merged = (_sigmoid(g_ref[:, 0:D_MODEL]) * oa_ref[...]
              + _sigmoid(g_ref[:, D_MODEL:2 * D_MODEL]) * ob_ref[...])
    h = h_ref[...] + jnp.dot(merged.astype(BF16), wout_ref[...], preferred_element_type=F32)
    xn = _rms(h, gain2_ref[...]).astype(BF16)
    h = h + 0.5 * _swiglu(xn, wgu_ref, wd_ref, act_ref)
    o_ref[...] = _rms(h, fgain_ref[...])


def _out_ffn2(h3, z3, oa, ob, wout, gain2, wgu, wd, fgain, seq, tm):
    bsz = h3.shape[0]
    const = dict(pipeline_mode=pl.Buffered(1))
    tile = lambda w, j: pl.BlockSpec((None, tm, w), lambda b, i: (b, i, j))
    return pl.pallas_call(
        _out_kernel,
        grid=(bsz, seq // tm),
        in_specs=[
            tile(D_MODEL, 0),
            tile(2 * D_MODEL, COL_G // (2 * D_MODEL)),
            tile(D_MODEL, 0),
            tile(D_MODEL, 0),
            pl.BlockSpec((D_MODEL, D_MODEL), lambda b, i: (0, 0), **const),
            pl.BlockSpec((1, D_MODEL), lambda b, i: (0, 0)),
            pl.BlockSpec((D_MODEL, 2 * D_FF), lambda b, i: (0, 0), **const),
            pl.BlockSpec((D_FF, D_MODEL), lambda b, i: (0, 0), **const),
            pl.BlockSpec((1, D_MODEL), lambda b, i: (0, 0)),
        ],
        out_specs=tile(D_MODEL, 0),
        out_shape=jax.ShapeDtypeStruct((bsz, seq, D_MODEL), F32),
        scratch_shapes=[pltpu.VMEM((tm, D_FF), BF16)],
        compiler_params=pltpu.CompilerParams(
            dimension_semantics=("parallel", "parallel"), vmem_limit_bytes=VMEM_LIMIT),
        name="out_ffn2",
    )(h3, z3, oa, ob, wout, gain2, wgu, wd, fgain)


def _regroup_in_weight(w):
    b0 = 4 * D_MODEL + 2 * A_HEADS
    lora0 = b0 + 3 * D_MODEL
    gate0 = lora0 + W_LORA + AA_LORA + G_LORA
    zeros = lambda n: jnp.zeros((w.shape[0], n), w.dtype)
    return jnp.concatenate([
        w[:, 0:4 * D_MODEL],
        w[:, gate0:gate0 + 2 * D_MODEL],
        w[:, b0:lora0],
        w[:, 4 * D_MODEL:b0], zeros(LANES - 2 * A_HEADS),
        w[:, lora0:lora0 + W_LORA + AA_LORA],
        w[:, lora0 + W_LORA + AA_LORA:gate0], zeros(2 * LANES - G_LORA),
    ], axis=1)


def kernel(x, meta_tokens, ffn1_norm, ffn1_w_gu, ffn1_w_down, mix_norm, w_in, a_conv_w, a_log_rate, a_dt_bias, a_out_norm, b_shift_mu, b_w0, b_w_up, b_a0, b_a_up, b_g_up, b_k_k, b_k_a, b_r_k, b_ln_gain, b_ln_bias, w_out, ffn2_norm, ffn2_w_gu, ffn2_w_down, final_norm):
    bsz, seq, _ = x.shape
    tp = seq + CHUNK
    row = lambda v: v.reshape(1, -1).astype(F32)

    tail = jnp.concatenate([jnp.zeros((PAD_ROWS, D_MODEL), x.dtype), meta_tokens.astype(x.dtype)], axis=0)

    l = 0
    ffn1_w = (row(ffn1_norm[l]), ffn1_w_gu[l].astype(BF16), ffn1_w_down[l].astype(BF16), row(mix_norm[l]))
    h1, u = _ffn1(x, *ffn1_w, tm=1024)
    h1, u = _ffn1_tail(tail, h1, u, *ffn1_w)
    z = _inproj(u.reshape(bsz * tp, D_MODEL), _regroup_in_weight(w_in[l]).astype(BF16), tm=1280, tn=2432)
    z3 = z.reshape(bsz, tp, IN_COLS)

    lane_pad = lambda v: jnp.pad(v.astype(F32), (A_HEADS, LANES - 2 * A_HEADS)).reshape(1, LANES)
    mu = b_shift_mu[l].astype(F32)
    mu_b = mu[:3 * D_MODEL].reshape(1, -1)
    mu_s = jnp.pad(mu[3 * D_MODEL:], (0, 2 * LANES - G_LORA)).reshape(1, -1)
    wup = jnp.pad(b_w_up[l].astype(F32), ((0, AA_LORA), (0, 0)))
    aup = jnp.pad(b_a_up[l].astype(F32), ((W_LORA, 0), (0, 0)))
    gup = jnp.pad(b_g_up[l].astype(F32), ((0, 2 * LANES - G_LORA), (0, 0)))
    o_a, o_b = _mixers(
        z3, a_conv_w[l].astype(F32), lane_pad(jnp.exp(a_log_rate[l].astype(F32))), lane_pad(a_dt_bias[l]),
        row(a_out_norm[l]), mu_b, mu_s, row(b_w0[l]), wup, row(b_a0[l]), aup, gup,
        row(b_k_k[l]), row(b_k_a[l]), row(b_r_k[l]), row(b_ln_gain[l]), row(b_ln_bias[l]))

    return _out_ffn2(h1, z3, o_a, o_b, w_out[l].astype(BF16),
                     row(ffn2_norm[l]), ffn2_w_gu[l].astype(BF16), ffn2_w_down[l].astype(BF16),
                     row(final_norm), seq=seq, tm=512)
```

```python
import numpy as np

import jax
import jax.numpy as jnp
from jax import lax
from jax.experimental import pallas as pl
from jax.experimental.pallas import tpu as pltpu

F32 = jnp.float32
BF16 = jnp.bfloat16

D_MODEL = 1024
N_META = 16
EPS = 1e-6
D_FF = 2816
CHUNK = 64
A_DK = 128
A_HEADS = 8
A_CONV = 4
B_N = 64
B_HEADS = 16
W_LORA = 64
AA_LORA = 64
G_LORA = 160
B_GN_EPS = B_N * 1e-5

PAD_ROWS = CHUNK - N_META
FF_CHUNK = 256
TAIL = 8
PAIR = 2 * CHUNK
LANES = 128

COL_A = 0
COL_G = 4096
COL_B = 6144
COL_S = 9216
S_BA = 0
S_WA = 128
S_GD = 256
S_COLS = 512
IN_COLS = 9728

VMEM_LIMIT = 56 * 1024 * 1024

NN = (((1,), (0,)), ((), ()))
NT = (((1,), (1,)), ((), ()))
TN = (((0,), (0,)), ((), ()))

M_EYE, M_BLK8, M_OFF16, M_OFF32, M_OFF64, M_STRICT, M_NSTRICT, M_INCL = range(8)


def _mm(a, b, dims=NN):
    return lax.dot_general(a.astype(BF16), b.astype(BF16), dims, preferred_element_type=F32)


def _cumsum_rows(tri, x):
    hi = x.astype(BF16)
    lo = (x - hi.astype(F32)).astype(BF16)
    return jnp.dot(tri, hi, preferred_element_type=F32) + jnp.dot(tri, lo, preferred_element_type=F32)


def _sigmoid(x):
    return 1.0 / (1.0 + jnp.exp(-x))


def _silu(x):
    return x * _sigmoid(x)


def _softplus(x):
    return jnp.maximum(x, 0.0) + jnp.log(1.0 + jnp.exp(-jnp.abs(x)))


def _rms(x, gain):
    return x * lax.rsqrt(jnp.mean(x * x, axis=-1, keepdims=True) + EPS) * gain


def _pair_constants():
    row = np.arange(CHUNK)[:, None]
    col = np.arange(PAIR)[None, :] % CHUNK
    blk = lambda s: (row >> s) == (col >> s)
    strict = row > col
    planes = [row == col, blk(3), blk(4) & ~blk(3), blk(5) & ~blk(4), ~blk(5),
              strict, -1.0 * strict, row >= col]
    lanes = np.arange(LANES)[None, :] < B_N
    halves = [np.broadcast_to(lanes, (CHUNK, LANES)), np.broadcast_to(~lanes, (CHUNK, LANES))]
    return (jnp.asarray(np.stack([np.asarray(p, np.float32) for p in planes]), BF16),
            jnp.asarray(np.stack([np.asarray(h, np.float32) for h in halves]), BF16))


def _tri_inv(ms, mask, half):
    mm = lambda xs, ys: [_mm(x, _stack(y, half)).astype(BF16) for x, y in zip(xs, ys)]

    n1 = [m * mask(M_BLK8) for m in ms]
    n2 = mm(n1, n1)
    n4 = mm(n2, n2)
    d = mm([mask(M_EYE) - a for a in n1], [mask(M_EYE) + a for a in n2])
    d = mm(d, [mask(M_EYE) + a for a in n4])
    for level in (M_OFF16, M_OFF32, M_OFF64):
        de = mm(d, [m * mask(level) for m in ms])
        d = [a - b for a, b in zip(d, mm(de, d))]
    return d


def _stack(x, half):
    xb = x.astype(BF16)
    return jnp.concatenate([xb * half(0), xb * half(1)], axis=0)


def _head_sum(x, left):
    s0 = jnp.sum(jnp.where(left, x, 0.0), axis=-1, keepdims=True)
    s1 = jnp.sum(jnp.where(left, 0.0, x), axis=-1, keepdims=True)
    return jnp.where(left, s0, s1)


def _swiglu(xn, wgu_ref, wd_ref, act_ref):
    for c in range(D_FF // FF_CHUNK):
        lo = c * FF_CHUNK
        g = jnp.dot(xn, wgu_ref[:, lo:lo + FF_CHUNK], preferred_element_type=F32)
        u = jnp.dot(xn, wgu_ref[:, D_FF + lo:D_FF + lo + FF_CHUNK], preferred_element_type=F32)
        act_ref[:, lo:lo + FF_CHUNK] = (_silu(g) * u).astype(BF16)
    return jnp.dot(act_ref[...], wd_ref[...], preferred_element_type=F32)


def _ffn1_rows(h, gain_ref, wgu_ref, wd_ref, mixgain_ref, act_ref):
    xn = _rms(h, gain_ref[...]).astype(BF16)
    h = h + 0.5 * _swiglu(xn, wgu_ref, wd_ref, act_ref)
    return h, _rms(h, mixgain_ref[...]).astype(BF16)


def _ffn1_kernel(h_ref, gain_ref, wgu_ref, wd_ref, mixgain_ref, o_ref, u_ref, act_ref):
    o_ref[...], u_ref[...] = _ffn1_rows(h_ref[...], gain_ref, wgu_ref, wd_ref, mixgain_ref, act_ref)


def _ffn1_tail_kernel(t_ref, gain_ref, wgu_ref, wd_ref, mixgain_ref, h1_hbm, u_hbm, o_ref, u_ref, act_ref):
    del h1_hbm, u_hbm
    h1, u = _ffn1_rows(t_ref[...], gain_ref, wgu_ref, wd_ref, mixgain_ref, act_ref)
    o_ref[...] = jnp.broadcast_to(h1[None], o_ref.shape)
    u_ref[...] = jnp.broadcast_to(u[None], u_ref.shape)


def _ffn1_weight_specs(index):
    const = dict(pipeline_mode=pl.Buffered(1))
    vec = pl.BlockSpec((1, D_MODEL), index)
    return [vec,
            pl.BlockSpec((D_MODEL, 2 * D_FF), index, **const),
            pl.BlockSpec((D_FF, D_MODEL), index, **const),
            vec]


def _ffn1(x, gain, wgu, wd, mixgain, tm):
    bsz, seq, _ = x.shape
    tile = pl.BlockSpec((None, tm, D_MODEL), lambda b, i: (b, i, 0))
    return pl.pallas_call(
        _ffn1_kernel,
        grid=(bsz, seq // tm),
        in_specs=[tile] + _ffn1_weight_specs(lambda b, i: (0, 0)),
        out_specs=[tile, tile],
        out_shape=[jax.ShapeDtypeStruct((bsz, seq + CHUNK, D_MODEL), F32),
                   jax.ShapeDtypeStruct((bsz, seq + CHUNK, D_MODEL), BF16)],
        scratch_shapes=[pltpu.VMEM((tm, D_FF), BF16)],
        compiler_params=pltpu.CompilerParams(
            dimension_semantics=("parallel", "parallel"), vmem_limit_bytes=VMEM_LIMIT),
        name="ffn1",
    )(x, gain, wgu, wd, mixgain)


def _ffn1_tail(tail, h1, u, gain, wgu, wd, mixgain):
    bsz, tp, _ = h1.shape
    last = pl.BlockSpec((bsz, CHUNK, D_MODEL), lambda i: (0, tp // CHUNK - 1, 0))
    return pl.pallas_call(
        _ffn1_tail_kernel,
        grid=(1,),
        in_specs=([pl.BlockSpec((CHUNK, D_MODEL), lambda i: (0, 0))] + _ffn1_weight_specs(lambda i: (0, 0))
                  + [pl.BlockSpec(memory_space=pl.ANY), pl.BlockSpec(memory_space=pl.ANY)]),
        out_specs=[last, last],
        out_shape=[jax.ShapeDtypeStruct(h1.shape, h1.dtype), jax.ShapeDtypeStruct(u.shape, u.dtype)],
        input_output_aliases={5: 0, 6: 1},
        scratch_shapes=[pltpu.VMEM((CHUNK, D_FF), BF16)],
        compiler_params=pltpu.CompilerParams(
            dimension_semantics=("arbitrary",), vmem_limit_bytes=VMEM_LIMIT),
        name="ffn1_tail",
    )(tail, gain, wgu, wd, mixgain, h1, u)


def _inproj_kernel(u_ref, w_ref, o_ref):
    o_ref[...] = jnp.dot(u_ref[...], w_ref[...], preferred_element_type=F32)


def _inproj(u, w, tm, tn):
    n = u.shape[0]
    return pl.pallas_call(
        _inproj_kernel,
        grid=(IN_COLS // tn, n // tm),
        in_specs=[
            pl.BlockSpec((tm, D_MODEL), lambda j, i: (i, 0)),
            pl.BlockSpec((D_MODEL, tn), lambda j, i: (0, j)),
        ],
        out_specs=pl.BlockSpec((tm, tn), lambda j, i: (i, j)),
        out_shape=jax.ShapeDtypeStruct((n, IN_COLS), F32),
        compiler_params=pltpu.CompilerParams(
            dimension_semantics=("parallel", "parallel"), vmem_limit_bytes=VMEM_LIMIT),
        name="in_proj",
    )(u, w)


ROWS = 2


def _mixer_kernel(za_ref, zb_ref, sm_ref, mask_ref, half_ref,
                  convw_ref, rate_ref, dtb_ref, again_ref,
                  mub_ref, mus_ref, w0_ref, wup_ref, a0_ref, aup_ref, gup_ref,
                  kk_ref, ka_ref, rk_ref, lng_ref, lnb_ref,
                  oa_ref, ob_ref,
                  exta_ref, extb_ref, exts_ref, sa_ref, sb_ref):
    c = pl.program_id(1)
    wqkv = 3 * D_MODEL
    ws = S_COLS - S_WA

    @pl.when(c == 0)
    def _():
        sa_ref[...] = jnp.zeros_like(sa_ref)
        sb_ref[...] = jnp.zeros_like(sb_ref)
        exta_ref[...] = jnp.zeros_like(exta_ref)
        extb_ref[...] = jnp.zeros_like(extb_ref)
        exts_ref[...] = jnp.zeros_like(exts_ref)

    mask = lambda i: mask_ref[i]
    half = lambda i: half_ref[i]
    left = lax.broadcasted_iota(jnp.int32, (CHUNK, LANES), 1) < B_N
    tri = lambda: mask_ref[M_INCL, 0:CHUNK, 0:CHUNK]
    incl2 = lambda: jnp.concatenate([mask(M_INCL), mask(M_INCL)], axis=1)
    apairs = range(A_HEADS // 2)
    bpairs = range(B_HEADS // 2)
    blanes = [slice(p * LANES, (p + 1) * LANES) for p in bpairs]
    prow = lax.broadcasted_iota(jnp.int32, (PAIR, PAIR), 0)
    pcol = lax.broadcasted_iota(jnp.int32, (PAIR, PAIR), 1)
    same = (prow >> 6) == (pcol >> 6)
    incl = (lax.broadcasted_iota(jnp.int32, (CHUNK, PAIR), 0)
            >= (lax.broadcasted_iota(jnp.int32, (CHUNK, PAIR), 1) & (CHUNK - 1)))
    lcat = lambda a, b: jnp.concatenate([a, b], axis=1)
    rcat = lambda a, b: jnp.concatenate([a, b], axis=0)

    def diag2(xa, xb):
        za, zb = jnp.zeros(xa.shape, BF16), jnp.zeros(xb.shape, BF16)
        return jnp.concatenate([jnp.concatenate([xa.astype(BF16), zb], axis=1),
                                jnp.concatenate([za, xb.astype(BF16)], axis=1)], axis=0)


    def prepare(bi, tie):
        hold = (lambda n: 0.0) if tie is None else (lambda n: jnp.tile(tie, (1, n // LANES)))
        o = {}
        def shifted(cur, ext_ref, back):
            ext = jnp.concatenate([ext_ref[bi], cur], axis=0)
            return pltpu.roll(ext, back, axis=0)[TAIL:, :]

        curb = zb_ref[bi]
        prevb = shifted(curb, extb_ref, 1)
        extb_ref[bi] = curb[CHUNK - TAIL:, :]
        zb = curb + (prevb - curb) * (mub_ref[...] + hold(wqkv))
        curs = sm_ref[bi, :, S_WA:S_COLS]
        prevs = shifted(curs, exts_ref, 1)
        exts_ref[bi] = curs[CHUNK - TAIL:, :]
        zs = curs + (prevs - curs) * (mus_ref[...] + hold(ws))

        r_all = zb[:, 0:D_MODEL]
        k_all = zb[:, D_MODEL:2 * D_MODEL]
        v_all = zb[:, 2 * D_MODEL:3 * D_MODEL]
        wa = zs[:, 0:LANES]
        gd = zs[:, LANES:3 * LANES]

        logw_all = -float(np.exp(-0.5)) * _sigmoid(w0_ref[...] + _mm(jnp.tanh(wa), wup_ref[...]))
        a_all = _sigmoid(a0_ref[...] + _mm(wa, aup_ref[...]))
        o["gate"] = _mm(_sigmoid(gd), gup_ref[...])
        kk_all = k_all * kk_ref[...]
        k2_all = k_all * (1.0 + (a_all - 1.0) * ka_ref[...])
        cum_all = _cumsum_rows(tri(), logw_all)

        cura = za_ref[bi, :, 0:wqkv]
        conv = (convw_ref[A_CONV - 1:A_CONV, :] + hold(wqkv)) * cura
        for j in range(A_CONV - 1):
            conv = conv + convw_ref[j:j + 1, :] * shifted(cura, exta_ref, A_CONV - 1 - j)
        exta_ref[bi] = cura[CHUNK - TAIL:, :]
        qkv = _silu(conv)

        sm = sm_ref[bi, :, S_BA:S_BA + LANES]
        rows = lax.broadcasted_iota(jnp.int32, (CHUNK, LANES), 0)
        real = jnp.logical_or(c != 0, rows >= PAD_ROWS)
        beta_all = jnp.where(real, _sigmoid(sm), 0.0)
        g_all = jnp.where(real, -rate_ref[...] * _softplus(sm + (dtb_ref[...] + hold(LANES))), 0.0)
        gcum = _cumsum_rows(tri(), g_all)
        gcum_t = gcum.T

        for name in ("r", "k2", "v", "bv", "c_last", "p_tail", "atrt", "v_st", "bk_st"):
            o[name] = []
        for p, sl in enumerate(blanes):
            kk = kk_all[:, sl]
            kk = kk * lax.rsqrt(_head_sum(kk * kk, left) + 1e-6)
            cum = cum_all[:, sl]
            o["r"].append(r_all[:, sl])
            o["k2"].append(k2_all[:, sl])
            o["v"].append(v_all[:, sl])
            o["bv"].append(kk * a_all[:, sl])
            o["c_last"].append(cum[CHUNK - 1:CHUNK, :])
            p_inv = jnp.exp(-cum)
            o["p_tail"].append(jnp.exp(o["c_last"][p] - cum))
            at = (-kk * jnp.exp(cum - logw_all[:, sl])).astype(BF16)
            rt = (o["r"][p] * jnp.exp(cum)).astype(BF16)
            o["atrt"].append(jnp.concatenate([at, rt], axis=0))
            o["v_st"].append(_stack(o["v"][p], half))
            o["bk_st"].append(jnp.concatenate([_stack(o["bv"][p] * p_inv, half),
                                               _stack(o["k2"][p] * p_inv, half)], axis=0))

        for name in ("g_last", "decay", "vbe", "qe", "al", "ar", "ktail"):
            o[name] = []
        for p in apairs:
            qs, ks, kbs, gcs = [], [], [], []
            for h in (2 * p, 2 * p + 1):
                lo = h * A_DK
                q = qkv[:, lo:lo + A_DK]
                k = qkv[:, D_MODEL + lo:D_MODEL + lo + A_DK]
                beta = beta_all[:, h:h + 1]
                gc_h = gcum[:, A_HEADS + h:A_HEADS + h + 1]
                eg = jnp.exp(gc_h)
                gcs.append(gc_h)
                qs.append(q * lax.rsqrt(jnp.sum(q * q, axis=-1, keepdims=True) + 1e-6) * (A_DK ** -0.5))
                ks.append(k * lax.rsqrt(jnp.sum(k * k, axis=-1, keepdims=True) + 1e-6))
                kbs.append(ks[-1] * beta)
                o["vbe"].append(lcat(qkv[:, 2 * D_MODEL + lo:2 * D_MODEL + lo + A_DK] * beta, kbs[-1] * eg))
                o["qe"].append(qs[-1] * eg)
                o["g_last"].append(gc_h[CHUNK - 1:CHUNK, :])
                o["ktail"].append(ks[-1] * jnp.exp(o["g_last"][-1] - gc_h))
            la, lb = A_HEADS + 2 * p, A_HEADS + 2 * p + 1
            gr_p = lcat(gcum_t[la:la + 1, :], gcum_t[lb:lb + 1, :])
            gc_p = jnp.where(left, gcs[0], gcs[1])
            o["decay"].append(jnp.where(incl, jnp.exp(jnp.where(incl, gc_p - gr_p, 0.0)), 0.0))
            o["al"].append(rcat(lcat(kbs[0], kbs[1]), lcat(qs[0], qs[1])))
            o["ar"].append(diag2(ks[0], ks[1]))
        return o

    def grams(bi, o):
        o["bgram"] = [_mm(o["atrt"][p], o["bk_st"][p], NT).astype(BF16) for p in bpairs]
        o["agram"] = [_mm(o["al"][p], o["ar"][p], NT) for p in apairs]
        o["x0"] = [_mm(o["atrt"][p], sb_ref[bi, p], NT) for p in bpairs]
        o["akv"] = [_mm(o["bgram"][p][:CHUNK, PAIR:] * mask(M_STRICT), o["v_st"][p]) for p in bpairs]

    def recur(bi, o):
        bgram, agram, x0 = o["bgram"], o["agram"], o["x0"]
        attn = [agram[p][CHUNK:] * o["decay"][p] for p in apairs]
        tinv = _tri_inv([(agram[p][:CHUNK] * o["decay"][p]).astype(BF16) * mask(M_STRICT) for p in apairs]
                        + [bgram[p][:CHUNK, :PAIR] * mask(M_NSTRICT) for p in bpairs], mask, half)
        atinv, btinv = tinv[:len(apairs)], tinv[len(apairs):]
        uw = [_mm(atinv[p], diag2(o["vbe"][2 * p], o["vbe"][2 * p + 1])) for p in apairs]
        u = [_mm(btinv[p], _stack(x0[p][:CHUNK] + o["akv"][p], half)) for p in bpairs]
        ws_qs = [_mm(rcat(uw[h // 2][:, (2 * (h % 2) + 1) * A_DK:(2 * (h % 2) + 2) * A_DK], o["qe"][h]),
                     sa_ref[bi, h]) for h in range(A_HEADS)]
        o["yy"] = [_mm(bgram[p][CHUNK:, :] * incl2(),
                       jnp.concatenate([_stack(u[p], half), o["v_st"][p]], axis=0)) for p in bpairs]
        o["bupd"] = [_mm(jnp.concatenate([u[p], o["v"][p]], axis=0),
                         jnp.concatenate([o["bv"][p] * o["p_tail"][p], o["k2"][p] * o["p_tail"][p]], axis=0), TN)
                     for p in bpairs]
        v_new = [uw[h // 2][:, 2 * (h % 2) * A_DK:(2 * (h % 2) + 1) * A_DK] - ws_qs[h][:CHUNK]
                 for h in range(A_HEADS)]
        o["ws_qs"] = ws_qs
        o["av"] = [_mm(attn[p], diag2(v_new[2 * p], v_new[2 * p + 1])) for p in apairs]
        o["aupd"] = [_mm(o["ktail"][h], v_new[h], TN) for h in range(A_HEADS)]

    def finish(bi, o):
        for p, sl in enumerate(blanes):
            sb_ref[bi, p] = sb_ref[bi, p] * jnp.exp(o["c_last"][p]) + jnp.where(same, o["bupd"][p], 0.0)
            y = o["x0"][p][CHUNK:] + o["yy"][p]
            mean = _head_sum(y, left) * (1.0 / B_N)
            yc = y - mean
            var = _head_sum(yc * yc, left) * (1.0 / B_N)
            yn = yc * lax.rsqrt(var + B_GN_EPS) * lng_ref[:, sl] + lnb_ref[:, sl]
            bonus = _head_sum(o["r"][p] * o["k2"][p] * rk_ref[:, sl], left) * o["v"][p]
            ob_ref[bi, :, sl] = (yn + bonus) * o["gate"][:, sl]
        for h in range(A_HEADS):
            lo = h * A_DK
            sa_ref[bi, h] = sa_ref[bi, h] * jnp.exp(o["g_last"][h]) + o["aupd"][h]
            out = o["ws_qs"][h][CHUNK:] + o["av"][h // 2][:, (h % 2) * A_DK:(h % 2 + 1) * A_DK]
            out = out * lax.rsqrt(jnp.mean(out * out, axis=-1, keepdims=True) + EPS) * again_ref[...]
            zg = za_ref[bi, :, 3 * D_MODEL + lo:3 * D_MODEL + lo + A_DK]
            oa_ref[bi, :, lo:lo + A_DK] = out * _silu(zg)

    ops = [prepare(0, None)]
    grams(0, ops[0])
    for bi in range(1, ROWS):
        anchor = ops[bi - 1]["bgram"][0][0:16, 0:LANES].astype(F32)[0:1]
        ops.append(prepare(bi, jnp.minimum(jnp.abs(anchor), 0.0)))
        recur(bi - 1, ops[bi - 1])
        grams(bi, ops[bi])
        finish(bi - 1, ops[bi - 1])
    recur(ROWS - 1, ops[ROWS - 1])
    finish(ROWS - 1, ops[ROWS - 1])


def _mixers(z3, convw, rate, dtb, again, mub, mus, w0, wup, a0, aup, gup, kk, ka, rk, lng, lnb):
    bsz, tp, _ = z3.shape
    nc = tp // CHUNK
    chunk = lambda w, j: pl.BlockSpec((ROWS, CHUNK, w), lambda b, c: (b, (c + nc - 1) % nc, j))
    const = lambda r, w: pl.BlockSpec((r, w), lambda b, c: (0, 0))
    out = jax.ShapeDtypeStruct((bsz, tp, D_MODEL), F32)
    masks, halves = _pair_constants()
    return pl.pallas_call(
        _mixer_kernel,
        grid=(bsz // ROWS, nc),
        in_specs=[
            chunk(4 * D_MODEL, COL_A // (4 * D_MODEL)),
            chunk(3 * D_MODEL, COL_B // (3 * D_MODEL)),
            chunk(S_COLS, COL_S // S_COLS),
            pl.BlockSpec(masks.shape, lambda b, c: (0, 0, 0)),
            pl.BlockSpec(halves.shape, lambda b, c: (0, 0, 0)),
            const(A_CONV, 3 * D_MODEL), const(1, LANES), const(1, LANES), const(1, A_DK),
            const(1, 3 * D_MODEL), const(1, S_COLS - S_WA),
            const(1, D_MODEL), const(LANES, D_MODEL), const(1, D_MODEL), const(LANES, D_MODEL),
            const(2 * LANES, D_MODEL),
            const(1, D_MODEL), const(1, D_MODEL), const(1, D_MODEL), const(1, D_MODEL), const(1, D_MODEL),
        ],
        out_specs=[chunk(D_MODEL, 0), chunk(D_MODEL, 0)],
        out_shape=[out, out],
        scratch_shapes=[
            pltpu.VMEM((ROWS, TAIL, 3 * D_MODEL), F32),
            pltpu.VMEM((ROWS, TAIL, 3 * D_MODEL), F32),
            pltpu.VMEM((ROWS, TAIL, S_COLS - S_WA), F32),
            pltpu.VMEM((ROWS, A_HEADS, A_DK, A_DK), F32),
            pltpu.VMEM((ROWS, B_HEADS // 2, PAIR, PAIR), F32),
        ],
        compiler_params=pltpu.CompilerParams(
            dimension_semantics=("parallel", "arbitrary"), vmem_limit_bytes=VMEM_LIMIT),
        name="mixers",
    )(z3, z3, z3, masks, halves, convw, rate, dtb, again, mub, mus, w0, wup, a0, aup, gup, kk, ka, rk, lng, lnb)


def _out_kernel(h_ref, g_ref, oa_ref, ob_ref, wout_ref, gain2_ref, wgu_ref, wd_ref, fgain_ref,
                o_ref, act_ref):
    merged = (_sigmoid(g_ref[:, 0:D_MODEL]) * oa_ref[...]
              + _sigmoid(g_ref[:, D_MODEL:2 * D_MODEL]) * ob_ref[...])
    h = h_ref[...] + jnp.dot(merged.astype(BF16), wout_ref[...], preferred_element_type=F32)
    xn = _rms(h, gain2_ref[...]).astype(BF16)
    h = h + 0.5 * _swiglu(xn, wgu_ref, wd_ref, act_ref)
    o_ref[...] = _rms(h, fgain_ref[...])


def _out_ffn2(h3, z3, oa, ob, wout, gain2, wgu, wd, fgain, seq, tm):
    bsz = h3.shape[0]
    const = dict(pipeline_mode=pl.Buffered(1))
    tile = lambda w, j: pl.BlockSpec((None, tm, w), lambda b, i: (b, i, j))
    return pl.pallas_call(
        _out_kernel,
        grid=(bsz, seq // tm),
        in_specs=[
            tile(D_MODEL, 0),
            tile(2 * D_MODEL, COL_G // (2 * D_MODEL)),
            tile(D_MODEL, 0),
            tile(D_MODEL, 0),
            pl.BlockSpec((D_MODEL, D_MODEL), lambda b, i: (0, 0), **const),
            pl.BlockSpec((1, D_MODEL), lambda b, i: (0, 0)),
            pl.BlockSpec((D_MODEL, 2 * D_FF), lambda b, i: (0, 0), **const),
            pl.BlockSpec((D_FF, D_MODEL), lambda b, i: (0, 0), **const),
            pl.BlockSpec((1, D_MODEL), lambda b, i: (0, 0)),
        ],
        out_specs=tile(D_MODEL, 0),
        out_shape=jax.ShapeDtypeStruct((bsz, seq, D_MODEL), F32),
        scratch_shapes=[pltpu.VMEM((tm, D_FF), BF16)],
        compiler_params=pltpu.CompilerParams(
            dimension_semantics=("parallel", "parallel"), vmem_limit_bytes=VMEM_LIMIT),
        name="out_ffn2",
    )(h3, z3, oa, ob, wout, gain2, wgu, wd, fgain)


def _regroup_in_weight(w):
    b0 = 4 * D_MODEL + 2 * A_HEADS
    lora0 = b0 + 3 * D_MODEL
    gate0 = lora0 + W_LORA + AA_LORA + G_LORA
    zeros = lambda n: jnp.zeros((w.shape[0], n), w.dtype)
    return jnp.concatenate([
        w[:, 0:4 * D_MODEL],
        w[:, gate0:gate0 + 2 * D_MODEL],
        w[:, b0:lora0],
        w[:, 4 * D_MODEL:b0], zeros(LANES - 2 * A_HEADS),
        w[:, lora0:lora0 + W_LORA + AA_LORA],
        w[:, lora0 + W_LORA + AA_LORA:gate0], zeros(2 * LANES - G_LORA),
    ], axis=1)


def kernel(x, meta_tokens, ffn1_norm, ffn1_w_gu, ffn1_w_down, mix_norm, w_in, a_conv_w, a_log_rate, a_dt_bias, a_out_norm, b_shift_mu, b_w0, b_w_up, b_a0, b_a_up, b_g_up, b_k_k, b_k_a, b_r_k, b_ln_gain, b_ln_bias, w_out, ffn2_norm, ffn2_w_gu, ffn2_w_down, final_norm):
    bsz, seq, _ = x.shape
    tp = seq + CHUNK
    row = lambda v: v.reshape(1, -1).astype(F32)

    tail = jnp.concatenate([jnp.zeros((PAD_ROWS, D_MODEL), x.dtype), meta_tokens.astype(x.dtype)], axis=0)

    l = 0
    ffn1_w = (row(ffn1_norm[l]), ffn1_w_gu[l].astype(BF16), ffn1_w_down[l].astype(BF16), row(mix_norm[l]))
    h1, u = _ffn1(x, *ffn1_w, tm=1024)
    h1, u = _ffn1_tail(tail, h1, u, *ffn1_w)
    z = _inproj(u.reshape(bsz * tp, D_MODEL), _regroup_in_weight(w_in[l]).astype(BF16), tm=1280, tn=2432)
    z3 = z.reshape(bsz, tp, IN_COLS)

    lane_pad = lambda v: jnp.pad(v.astype(F32), (A_HEADS, LANES - 2 * A_HEADS)).reshape(1, LANES)
    mu = b_shift_mu[l].astype(F32)
    mu_b = mu[:3 * D_MODEL].reshape(1, -1)
    mu_s = jnp.pad(mu[3 * D_MODEL:], (0, 2 * LANES - G_LORA)).reshape(1, -1)
    wup = jnp.pad(b_w_up[l].astype(F32), ((0, AA_LORA), (0, 0)))
    aup = jnp.pad(b_a_up[l].astype(F32), ((W_LORA, 0), (0, 0)))
    gup = jnp.pad(b_g_up[l].astype(F32), ((0, 2 * LANES - G_LORA), (0, 0)))
    o_a, o_b = _mixers(
        z3, a_conv_w[l].astype(F32), lane_pad(jnp.exp(a_log_rate[l].astype(F32))), lane_pad(a_dt_bias[l]),
        row(a_out_norm[l]), mu_b, mu_s, row(b_w0[l]), wup, row(b_a0[l]), aup, gup,
        row(b_k_k[l]), row(b_k_a[l]), row(b_r_k[l]), row(b_ln_gain[l]), row(b_ln_bias[l]))

    return _out_ffn2(h1, z3, o_a, o_b, w_out[l].astype(BF16),
                     row(ffn2_norm[l]), ffn2_w_gu[l].astype(BF16), ffn2_w_down[l].astype(BF16),
                     row(final_norm), seq=seq, tm=512)
```

```python
import numpy as np

import jax
import jax.numpy as jnp
from jax import lax
from jax.experimental import pallas as pl
from jax.experimental.pallas import tpu as pltpu

F32 = jnp.float32
BF16 = jnp.bfloat16

D_MODEL = 1024
N_META = 16
EPS = 1e-6
D_FF = 2816
CHUNK = 64
A_DK = 128
A_HEADS = 8
A_CONV = 4
B_N = 64
B_HEADS = 16
W_LORA = 64
AA_LORA = 64
G_LORA = 160
B_GN_EPS = B_N * 1e-5

PAD_ROWS = CHUNK - N_META
FF_CHUNK = 256
TAIL = 8
PAIR = 2 * CHUNK
LANES = 128

COL_A = 0
COL_G = 4096
COL_B = 6144
COL_S = 9216
S_BA = 0
S_WA = 128
S_GD = 256
S_COLS = 512
IN_COLS = 9728

VMEM_LIMIT = 56 * 1024 * 1024

NN = (((1,), (0,)), ((), ()))
NT = (((1,), (1,)), ((), ()))
TN = (((0,), (0,)), ((), ()))

M_EYE, M_BLK8, M_OFF16, M_OFF32, M_OFF64, M_STRICT, M_NSTRICT, M_INCL = range(8)


def _mm(a, b, dims=NN):
    return lax.dot_general(a.astype(BF16), b.astype(BF16), dims, preferred_element_type=F32)


def _cumsum_rows(tri, x):
    hi = x.astype(BF16)
    lo = (x - hi.astype(F32)).astype(BF16)
    return jnp.dot(tri, hi, preferred_element_type=F32) + jnp.dot(tri, lo, preferred_element_type=F32)


def _sigmoid(x):
    return 1.0 / (1.0 + jnp.exp(-x))


def _silu(x):
    return x * _sigmoid(x)


def _softplus(x):
    return jnp.maximum(x, 0.0) + jnp.log(1.0 + jnp.exp(-jnp.abs(x)))


def _rms(x, gain):
    return x * lax.rsqrt(jnp.mean(x * x, axis=-1, keepdims=True) + EPS) * gain


def _pair_constants():
    row = np.arange(CHUNK)[:, None]
    col = np.arange(PAIR)[None, :] % CHUNK
    blk = lambda s: (row >> s) == (col >> s)
    strict = row > col
    planes = [row == col, blk(3), blk(4) & ~blk(3), blk(5) & ~blk(4), ~blk(5),
              strict, -1.0 * strict, row >= col]
    lanes = np.arange(LANES)[None, :] < B_N
    halves = [np.broadcast_to(lanes, (CHUNK, LANES)), np.broadcast_to(~lanes, (CHUNK, LANES))]
    return (jnp.asarray(np.stack([np.asarray(p, np.float32) for p in planes]), BF16),
            jnp.asarray(np.stack([np.asarray(h, np.float32) for h in halves]), BF16))


def _tri_inv(ms, mask, half):
    mm = lambda xs, ys: [_mm(x, _stack(y, half)).astype(BF16) for x, y in zip(xs, ys)]

    n1 = [m * mask(M_BLK8) for m in ms]
    n2 = mm(n1, n1)
    n4 = mm(n2, n2)
    d = mm([mask(M_EYE) - a for a in n1], [mask(M_EYE) + a for a in n2])
    d = mm(d, [mask(M_EYE) + a for a in n4])
    for level in (M_OFF16, M_OFF32, M_OFF64):
        de = mm(d, [m * mask(level) for m in ms])
        d = [a - b for a, b in zip(d, mm(de, d))]
    return d


def _stack(x, half):
    xb = x.astype(BF16)
    return jnp.concatenate([xb * half(0), xb * half(1)], axis=0)


def _head_sum(x, left):
    s0 = jnp.sum(jnp.where(left, x, 0.0), axis=-1, keepdims=True)
    s1 = jnp.sum(jnp.where(left, 0.0, x), axis=-1, keepdims=True)
    return jnp.where(left, s0, s1)


def _swiglu(xn, wgu_ref, wd_ref, act_ref):
    for c in range(D_FF // FF_CHUNK):
        lo = c * FF_CHUNK
        g = jnp.dot(xn, wgu_ref[:, lo:lo + FF_CHUNK], preferred_element_type=F32)
        u = jnp.dot(xn, wgu_ref[:, D_FF + lo:D_FF + lo + FF_CHUNK], preferred_element_type=F32)
        act_ref[:, lo:lo + FF_CHUNK] = (_silu(g) * u).astype(BF16)
    return jnp.dot(act_ref[...], wd_ref[...], preferred_element_type=F32)


def _ffn1_rows(h, gain_ref, wgu_ref, wd_ref, mixgain_ref, act_ref):
    xn = _rms(h, gain_ref[...]).astype(BF16)
    h = h + 0.5 * _swiglu(xn, wgu_ref, wd_ref, act_ref)
    return h, _rms(h, mixgain_ref[...]).astype(BF16)


def _ffn1_kernel(h_ref, gain_ref, wgu_ref, wd_ref, mixgain_ref, o_ref, u_ref, act_ref):
    last = pl.program_id(1) == pl.num_programs(1) - 1

    @pl.when(jnp.logical_not(last))
    def _():
        o_ref[...], u_ref[...] = _ffn1_rows(h_ref[...], gain_ref, wgu_ref, wd_ref, mixgain_ref, act_ref)

    @pl.when(last)
    def _():
        o_ref[...] = jnp.zeros_like(o_ref)
        u_ref[...] = jnp.zeros_like(u_ref)


def _ffn1_tail_kernel(t_ref, gain_ref, wgu_ref, wd_ref, mixgain_ref, h1_hbm, u_hbm, o_ref, u_ref, act_ref):
    del h1_hbm, u_hbm
    h1, u = _ffn1_rows(t_ref[...], gain_ref, wgu_ref, wd_ref, mixgain_ref, act_ref)
    o_ref[...] = jnp.broadcast_to(h1[None], o_ref.shape)
    u_ref[...] = jnp.broadcast_to(u[None], u_ref.shape)


def _ffn1_weight_specs(index):
    const = dict(pipeline_mode=pl.Buffered(1))
    vec = pl.BlockSpec((1, D_MODEL), index)
    return [vec,
            pl.BlockSpec((D_MODEL, 2 * D_FF), index, **const),
            pl.BlockSpec((D_FF, D_MODEL), index, **const),
            vec]


def _ffn1(x, gain, wgu, wd, mixgain, tm):
    bsz, seq, _ = x.shape
    nt = seq // tm
    tile = pl.BlockSpec((None, tm, D_MODEL), lambda b, i: (b, i, 0))
    return pl.pallas_call(
        _ffn1_kernel,
        grid=(bsz, nt + 1),
        in_specs=([pl.BlockSpec((None, tm, D_MODEL), lambda b, i: (b, jnp.minimum(i, nt - 1), 0))]
                  + _ffn1_weight_specs(lambda b, i: (0, 0))),
        out_specs=[tile, tile],
        out_shape=[jax.ShapeDtypeStruct((bsz, seq + CHUNK, D_MODEL), F32),
                   jax.ShapeDtypeStruct((bsz, seq + CHUNK, D_MODEL), BF16)],
        scratch_shapes=[pltpu.VMEM((tm, D_FF), BF16)],
        compiler_params=pltpu.CompilerParams(
            dimension_semantics=("parallel", "parallel"), vmem_limit_bytes=VMEM_LIMIT),
        name="ffn1",
    )(x, gain, wgu, wd, mixgain)


def _ffn1_tail(tail, h1, u, gain, wgu, wd, mixgain):
    bsz, tp, _ = h1.shape
    last = pl.BlockSpec((bsz, CHUNK, D_MODEL), lambda i: (0, tp // CHUNK - 1, 0))
    return pl.pallas_call(
        _ffn1_tail_kernel,
        grid=(1,),
        in_specs=([pl.BlockSpec((CHUNK, D_MODEL), lambda i: (0, 0))] + _ffn1_weight_specs(lambda i: (0, 0))
                  + [pl.BlockSpec(memory_space=pl.ANY), pl.BlockSpec(memory_space=pl.ANY)]),
        out_specs=[last, last],
        out_shape=[jax.ShapeDtypeStruct(h1.shape, h1.dtype), jax.ShapeDtypeStruct(u.shape, u.dtype)],
        input_output_aliases={5: 0, 6: 1},
        scratch_shapes=[pltpu.VMEM((CHUNK, D_FF), BF16)],
        compiler_params=pltpu.CompilerParams(
            dimension_semantics=("arbitrary",), vmem_limit_bytes=VMEM_LIMIT),
        name="ffn1_tail",
    )(tail, gain, wgu, wd, mixgain, h1, u)


def _inproj_kernel(u_ref, w_ref, o_ref):
    o_ref[...] = jnp.dot(u_ref[...], w_ref[...], preferred_element_type=F32)


def _inproj(u, w, tm, tn):
    n = u.shape[0]
    return pl.pallas_call(
        _inproj_kernel,
        grid=(IN_COLS // tn, n // tm),
        in_specs=[
            pl.BlockSpec((tm, D_MODEL), lambda j, i: (i, 0)),
            pl.BlockSpec((D_MODEL, tn), lambda j, i: (0, j)),
        ],
        out_specs=pl.BlockSpec((tm, tn), lambda j, i: (i, j)),
        out_shape=jax.ShapeDtypeStruct((n, IN_COLS), F32),
        compiler_params=pltpu.CompilerParams(
            dimension_semantics=("parallel", "parallel"), vmem_limit_bytes=VMEM_LIMIT),
        name="in_proj",
    )(u, w)


ROWS = 2


def _mixer_kernel(za_ref, zb_ref, sm_ref, mask_ref, half_ref,
                  convw_ref, rate_ref, dtb_ref, again_ref,
                  mub_ref, mus_ref, w0_ref, wup_ref, a0_ref, aup_ref, gup_ref,
                  kk_ref, ka_ref, rk_ref, lng_ref, lnb_ref,
                  oa_ref, ob_ref,
                  exta_ref, extb_ref, exts_ref, sa_ref, sb_ref):
    c = pl.program_id(1)
    wqkv = 3 * D_MODEL
    ws = S_COLS - S_WA

    @pl.when(c == 0)
    def _():
        sa_ref[...] = jnp.zeros_like(sa_ref)
        sb_ref[...] = jnp.zeros_like(sb_ref)
        exta_ref[...] = jnp.zeros_like(exta_ref)
        extb_ref[...] = jnp.zeros_like(extb_ref)
        exts_ref[...] = jnp.zeros_like(exts_ref)

    mask = lambda i: mask_ref[i]
    half = lambda i: half_ref[i]
    left = lax.broadcasted_iota(jnp.int32, (CHUNK, LANES), 1) < B_N
    tri = lambda: mask_ref[M_INCL, 0:CHUNK, 0:CHUNK]
    incl2 = lambda: jnp.concatenate([mask(M_INCL), mask(M_INCL)], axis=1)
    apairs = range(A_HEADS // 2)
    bpairs = range(B_HEADS // 2)
    blanes = [slice(p * LANES, (p + 1) * LANES) for p in bpairs]
    prow = lax.broadcasted_iota(jnp.int32, (PAIR, PAIR), 0)
    pcol = lax.broadcasted_iota(jnp.int32, (PAIR, PAIR), 1)
    same = (prow >> 6) == (pcol >> 6)
    incl = (lax.broadcasted_iota(jnp.int32, (CHUNK, PAIR), 0)
            >= (lax.broadcasted_iota(jnp.int32, (CHUNK, PAIR), 1) & (CHUNK - 1)))
    lcat = lambda a, b: jnp.concatenate([a, b], axis=1)
    rcat = lambda a, b: jnp.concatenate([a, b], axis=0)

    def diag2(xa, xb):
        za, zb = jnp.zeros(xa.shape, BF16), jnp.zeros(xb.shape, BF16)
        return jnp.concatenate([jnp.concatenate([xa.astype(BF16), zb], axis=1),
                                jnp.concatenate([za, xb.astype(BF16)], axis=1)], axis=0)


    def prepare(bi, tie):
        hold = (lambda n: 0.0) if tie is None else (lambda n: jnp.tile(tie, (1, n // LANES)))
        o = {}
        def shifted(cur, ext_ref, back):
            ext = jnp.concatenate([ext_ref[bi], cur], axis=0)
            return pltpu.roll(ext, back, axis=0)[TAIL:, :]

        curb = zb_ref[bi]
        prevb = shifted(curb, extb_ref, 1)
        extb_ref[bi] = curb[CHUNK - TAIL:, :]
        zb = curb + (prevb - curb) * (mub_ref[...] + hold(wqkv))
        curs = sm_ref[bi, :, S_WA:S_COLS]
        prevs = shifted(curs, exts_ref, 1)
        exts_ref[bi] = curs[CHUNK - TAIL:, :]
        zs = curs + (prevs - curs) * (mus_ref[...] + hold(ws))

        r_all = zb[:, 0:D_MODEL]
        k_all = zb[:, D_MODEL:2 * D_MODEL]
        v_all = zb[:, 2 * D_MODEL:3 * D_MODEL]
        wa = zs[:, 0:LANES]
        gd = zs[:, LANES:3 * LANES]

        logw_all = -float(np.exp(-0.5)) * _sigmoid(w0_ref[...] + _mm(jnp.tanh(wa), wup_ref[...]))
        a_all = _sigmoid(a0_ref[...] + _mm(wa, aup_ref[...]))
        o["gate"] = _mm(_sigmoid(gd), gup_ref[...])
        kk_all = k_all * kk_ref[...]
        k2_all = k_all * (1.0 + (a_all - 1.0) * ka_ref[...])
        cum_all = _cumsum_rows(tri(), logw_all)

        cura = za_ref[bi, :, 0:wqkv]
        conv = (convw_ref[A_CONV - 1:A_CONV, :] + hold(wqkv)) * cura
        for j in range(A_CONV - 1):
            conv = conv + convw_ref[j:j + 1, :] * shifted(cura, exta_ref, A_CONV - 1 - j)
        exta_ref[bi] = cura[CHUNK - TAIL:, :]
        qkv = _silu(conv)

        sm = sm_ref[bi, :, S_BA:S_BA + LANES]
        rows = lax.broadcasted_iota(jnp.int32, (CHUNK, LANES), 0)
        real = jnp.logical_or(c != 0, rows >= PAD_ROWS)
        beta_all = jnp.where(real, _sigmoid(sm), 0.0)
        g_all = jnp.where(real, -rate_ref[...] * _softplus(sm + (dtb_ref[...] + hold(LANES))), 0.0)
        gcum = _cumsum_rows(tri(), g_all)
        gcum_t = gcum.T

        for name in ("r", "k2", "v", "bv", "c_last", "p_tail", "atrt", "v_st", "bk_st"):
            o[name] = []
        for p, sl in enumerate(blanes):
            kk = kk_all[:, sl]
            kk = kk * lax.rsqrt(_head_sum(kk * kk, left) + 1e-6)
            cum = cum_all[:, sl]
            o["r"].append(r_all[:, sl])
            o["k2"].append(k2_all[:, sl])
            o["v"].append(v_all[:, sl])
            o["bv"].append(kk * a_all[:, sl])
            o["c_last"].append(cum[CHUNK - 1:CHUNK, :])
            p_inv = jnp.exp(-cum)
            o["p_tail"].append(jnp.exp(o["c_last"][p] - cum))
            at = (-kk * jnp.exp(cum - logw_all[:, sl])).astype(BF16)
            rt = (o["r"][p] * jnp.exp(cum)).astype(BF16)
            o["atrt"].append(jnp.concatenate([at, rt], axis=0))
            o["v_st"].append(_stack(o["v"][p], half))
            o["bk_st"].append(jnp.concatenate([_stack(o["bv"][p] * p_inv, half),
                                               _stack(o["k2"][p] * p_inv, half)], axis=0))

        for name in ("g_last", "decay", "vbe", "qe", "al", "ar", "ktail"):
            o[name] = []
        for p in apairs:
            qs, ks, kbs, gcs = [], [], [], []
            for h in (2 * p, 2 * p + 1):
                lo = h * A_DK
                q = qkv[:, lo:lo + A_DK]
                k = qkv[:, D_MODEL + lo:D_MODEL + lo + A_DK]
                beta = beta_all[:, h:h + 1]
                gc_h = gcum[:, A_HEADS + h:A_HEADS + h + 1]
                eg = jnp.exp(gc_h)
                gcs.append(gc_h)
                qs.append(q * lax.rsqrt(jnp.sum(q * q, axis=-1, keepdims=True) + 1e-6) * (A_DK ** -0.5))
                ks.append(k * lax.rsqrt(jnp.sum(k * k, axis=-1, keepdims=True) + 1e-6))
                kbs.append(ks[-1] * beta)
                o["vbe"].append(lcat(qkv[:, 2 * D_MODEL + lo:2 * D_MODEL + lo + A_DK] * beta, kbs[-1] * eg))
                o["qe"].append(qs[-1] * eg)
                o["g_last"].append(gc_h[CHUNK - 1:CHUNK, :])
                o["ktail"].append(ks[-1] * jnp.exp(o["g_last"][-1] - gc_h))
            la, lb = A_HEADS + 2 * p, A_HEADS + 2 * p + 1
            gr_p = lcat(gcum_t[la:la + 1, :], gcum_t[lb:lb + 1, :])
            gc_p = jnp.where(left, gcs[0], gcs[1])
            o["decay"].append(jnp.where(incl, jnp.exp(jnp.where(incl, gc_p - gr_p, 0.0)), 0.0))
            o["al"].append(rcat(lcat(kbs[0], kbs[1]), lcat(qs[0], qs[1])))
            o["ar"].append(diag2(ks[0], ks[1]))
        return o

    def grams(bi, o):
        o["bgram"] = [_mm(o["atrt"][p], o["bk_st"][p], NT).astype(BF16) for p in bpairs]
        o["agram"] = [_mm(o["al"][p], o["ar"][p], NT) for p in apairs]
        o["x0"] = [_mm(o["atrt"][p], sb_ref[bi, p], NT) for p in bpairs]
        o["akv"] = [_mm(o["bgram"][p][:CHUNK, PAIR:] * mask(M_STRICT), o["v_st"][p]) for p in bpairs]

    def recur(bi, o):
        bgram, agram, x0 = o["bgram"], o["agram"], o["x0"]
        attn = [agram[p][CHUNK:] * o["decay"][p] for p in apairs]
        tinv = _tri_inv([(agram[p][:CHUNK] * o["decay"][p]).astype(BF16) * mask(M_STRICT) for p in apairs]
                        + [bgram[p][:CHUNK, :PAIR] * mask(M_NSTRICT) for p in bpairs], mask, half)
        atinv, btinv = tinv[:len(apairs)], tinv[len(apairs):]
        uw = [_mm(atinv[p], diag2(o["vbe"][2 * p], o["vbe"][2 * p + 1])) for p in apairs]
        u = [_mm(btinv[p], _stack(x0[p][:CHUNK] + o["akv"][p], half)) for p in bpairs]
        ws_qs = [_mm(rcat(uw[h // 2][:, (2 * (h % 2) + 1) * A_DK:(2 * (h % 2) + 2) * A_DK], o["qe"][h]),
                     sa_ref[bi, h]) for h in range(A_HEADS)]
        o["yy"] = [_mm(bgram[p][CHUNK:, :] * incl2(),
                       jnp.concatenate([_stack(u[p], half), o["v_st"][p]], axis=0)) for p in bpairs]
        o["bupd"] = [_mm(jnp.concatenate([u[p], o["v"][p]], axis=0),
                         jnp.concatenate([o["bv"][p] * o["p_tail"][p], o["k2"][p] * o["p_tail"][p]], axis=0), TN)
                     for p in bpairs]
        v_new = [uw[h // 2][:, 2 * (h % 2) * A_DK:(2 * (h % 2) + 1) * A_DK] - ws_qs[h][:CHUNK]
                 for h in range(A_HEADS)]
        o["ws_qs"] = ws_qs
        o["av"] = [_mm(attn[p], diag2(v_new[2 * p], v_new[2 * p + 1])) for p in apairs]
        o["aupd"] = [_mm(o["ktail"][h], v_new[h], TN) for h in range(A_HEADS)]

    def finish(bi, o):
        for p, sl in enumerate(blanes):
            sb_ref[bi, p] = sb_ref[bi, p] * jnp.exp(o["c_last"][p]) + jnp.where(same, o["bupd"][p], 0.0)
            y = o["x0"][p][CHUNK:] + o["yy"][p]
            mean = _head_sum(y, left) * (1.0 / B_N)
            yc = y - mean
            var = _head_sum(yc * yc, left) * (1.0 / B_N)
            yn = yc * lax.rsqrt(var + B_GN_EPS) * lng_ref[:, sl] + lnb_ref[:, sl]
            bonus = _head_sum(o["r"][p] * o["k2"][p] * rk_ref[:, sl], left) * o["v"][p]
            ob_ref[bi, :, sl] = (yn + bonus) * o["gate"][:, sl]
        for h in range(A_HEADS):
            lo = h * A_DK
            sa_ref[bi, h] = sa_ref[bi, h] * jnp.exp(o["g_last"][h]) + o["aupd"][h]
            out = o["ws_qs"][h][CHUNK:] + o["av"][h // 2][:, (h % 2) * A_DK:(h % 2 + 1) * A_DK]
            out = out * lax.rsqrt(jnp.mean(out * out, axis=-1, keepdims=True) + EPS) * again_ref[...]
            zg = za_ref[bi, :, 3 * D_MODEL + lo:3 * D_MODEL + lo + A_DK]
            oa_ref[bi, :, lo:lo + A_DK] = out * _silu(zg)

    ops = [prepare(0, None)]
    grams(0, ops[0])
    for bi in range(1, ROWS):
        anchor = ops[bi - 1]["bgram"][0][0:16, 0:LANES].astype(F32)[0:1]
        ops.append(prepare(bi, jnp.minimum(jnp.abs(anchor), 0.0)))
        recur(bi - 1, ops[bi - 1])
        grams(bi, ops[bi])
        finish(bi - 1, ops[bi - 1])
    recur(ROWS - 1, ops[ROWS - 1])
    finish(ROWS - 1, ops[ROWS - 1])


def _mixers(z3, convw, rate, dtb, again, mub, mus, w0, wup, a0, aup, gup, kk, ka, rk, lng, lnb):
    bsz, tp, _ = z3.shape
    nc = tp // CHUNK
    chunk = lambda w, j: pl.BlockSpec((ROWS, CHUNK, w), lambda b, c: (b, (c + nc - 1) % nc, j))
    const = lambda r, w: pl.BlockSpec((r, w), lambda b, c: (0, 0))
    out = jax.ShapeDtypeStruct((bsz, tp, D_MODEL), F32)
    masks, halves = _pair_constants()
    return pl.pallas_call(
        _mixer_kernel,
        grid=(bsz // ROWS, nc),
        in_specs=[
            chunk(4 * D_MODEL, COL_A // (4 * D_MODEL)),
            chunk(3 * D_MODEL, COL_B // (3 * D_MODEL)),
            chunk(S_COLS, COL_S // S_COLS),
            pl.BlockSpec(masks.shape, lambda b, c: (0, 0, 0)),
            pl.BlockSpec(halves.shape, lambda b, c: (0, 0, 0)),
            const(A_CONV, 3 * D_MODEL), const(1, LANES), const(1, LANES), const(1, A_DK),
            const(1, 3 * D_MODEL), const(1, S_COLS - S_WA),
            const(1, D_MODEL), const(LANES, D_MODEL), const(1, D_MODEL), const(LANES, D_MODEL),
            const(2 * LANES, D_MODEL),
            const(1, D_MODEL), const(1, D_MODEL), const(1, D_MODEL), const(1, D_MODEL), const(1, D_MODEL),
        ],
        out_specs=[chunk(D_MODEL, 0), chunk(D_MODEL, 0)],
        out_shape=[out, out],
        scratch_shapes=[
            pltpu.VMEM((ROWS, TAIL, 3 * D_MODEL), F32),
            pltpu.VMEM((ROWS, TAIL, 3 * D_MODEL), F32),
            pltpu.VMEM((ROWS, TAIL, S_COLS - S_WA), F32),
            pltpu.VMEM((ROWS, A_HEADS, A_DK, A_DK), F32),
            pltpu.VMEM((ROWS, B_HEADS // 2, PAIR, PAIR), F32),
        ],
        compiler_params=pltpu.CompilerParams(
            dimension_semantics=("parallel", "arbitrary"), vmem_limit_bytes=VMEM_LIMIT),
        name="mixers",
    )(z3, z3, z3, masks, halves, convw, rate, dtb, again, mub, mus, w0, wup, a0, aup, gup, kk, ka, rk, lng, lnb)


def _out_kernel(h_ref, g_ref, oa_ref, ob_ref, wout_ref, gain2_ref, wgu_ref, wd_ref, fgain_ref,
                o_ref, act_ref):
    merged = (_sigmoid(g_ref[:, 0:D_MODEL]) * oa_ref[...]
              + _sigmoid(g_ref[:, D_MODEL:2 * D_MODEL]) * ob_ref[...])
    h = h_ref[...] + jnp.dot(merged.astype(BF16), wout_ref[...], preferred_element_type=F32)
    xn = _rms(h, gain2_ref[...]).astype(BF16)
    h = h + 0.5 * _swiglu(xn, wgu_ref, wd_ref, act_ref)
    o_ref[...] = _rms(h, fgain_ref[...])


def _out_ffn2(h3, z3, oa, ob, wout, gain2, wgu, wd, fgain, seq, tm):
    bsz = h3.shape[0]
    const = dict(pipeline_mode=pl.Buffered(1))
    tile = lambda w, j: pl.BlockSpec((None, tm, w), lambda b, i: (b, i, j))
    return pl.pallas_call(
        _out_kernel,
        grid=(bsz, seq // tm),
        in_specs=[
            tile(D_MODEL, 0),
            tile(2 * D_MODEL, COL_G // (2 * D_MODEL)),
            tile(D_MODEL, 0),
            tile(D_MODEL, 0),
            pl.BlockSpec((D_MODEL, D_MODEL), lambda b, i: (0, 0), **const),
            pl.BlockSpec((1, D_MODEL), lambda b, i: (0, 0)),
            pl.BlockSpec((D_MODEL, 2 * D_FF), lambda b, i: (0, 0), **const),
            pl.BlockSpec((D_FF, D_MODEL), lambda b, i: (0, 0), **const),
            pl.BlockSpec((1, D_MODEL), lambda b, i: (0, 0)),
        ],
        out_specs=tile(D_MODEL, 0),
        out_shape=jax.ShapeDtypeStruct((bsz, seq, D_MODEL), F32),
        scratch_shapes=[pltpu.VMEM((tm, D_FF), BF16)],
        compiler_params=pltpu.CompilerParams(
            dimension_semantics=("parallel", "parallel"), vmem_limit_bytes=VMEM_LIMIT),
        name="out_ffn2",
    )(h3, z3, oa, ob, wout, gain2, wgu, wd, fgain)


def _regroup_in_weight(w):
    b0 = 4 * D_MODEL + 2 * A_HEADS
    lora0 = b0 + 3 * D_MODEL
    gate0 = lora0 + W_LORA + AA_LORA + G_LORA
    zeros = lambda n: jnp.zeros((w.shape[0], n), w.dtype)
    return jnp.concatenate([
        w[:, 0:4 * D_MODEL],
        w[:, gate0:gate0 + 2 * D_MODEL],
        w[:, b0:lora0],
        w[:, 4 * D_MODEL:b0], zeros(LANES - 2 * A_HEADS),
        w[:, lora0:lora0 + W_LORA + AA_LORA],
        w[:, lora0 + W_LORA + AA_LORA:gate0], zeros(2 * LANES - G_LORA),
    ], axis=1)


def kernel(x, meta_tokens, ffn1_norm, ffn1_w_gu, ffn1_w_down, mix_norm, w_in, a_conv_w, a_log_rate, a_dt_bias, a_out_norm, b_shift_mu, b_w0, b_w_up, b_a0, b_a_up, b_g_up, b_k_k, b_k_a, b_r_k, b_ln_gain, b_ln_bias, w_out, ffn2_norm, ffn2_w_gu, ffn2_w_down, final_norm):
    bsz, seq, _ = x.shape
    tp = seq + CHUNK
    row = lambda v: v.reshape(1, -1).astype(F32)

    tail = jnp.concatenate([jnp.zeros((PAD_ROWS, D_MODEL), x.dtype), meta_tokens.astype(x.dtype)], axis=0)

    l = 0
    ffn1_w = (row(ffn1_norm[l]), ffn1_w_gu[l].astype(BF16), ffn1_w_down[l].astype(BF16), row(mix_norm[l]))
    h1, u = _ffn1(x, *ffn1_w, tm=1024)
    h1, u = _ffn1_tail(tail, h1, u, *ffn1_w)
    z = _inproj(u.reshape(bsz * tp, D_MODEL), _regroup_in_weight(w_in[l]).astype(BF16), tm=1280, tn=2432)
    z3 = z.reshape(bsz, tp, IN_COLS)

    lane_pad = lambda v: jnp.pad(v.astype(F32), (A_HEADS, LANES - 2 * A_HEADS)).reshape(1, LANES)
    mu = b_shift_mu[l].astype(F32)
    mu_b = mu[:3 * D_MODEL].reshape(1, -1)
    mu_s = jnp.pad(mu[3 * D_MODEL:], (0, 2 * LANES - G_LORA)).reshape(1, -1)
    wup = jnp.pad(b_w_up[l].astype(F32), ((0, AA_LORA), (0, 0)))
    aup = jnp.pad(b_a_up[l].astype(F32), ((W_LORA, 0), (0, 0)))
    gup = jnp.pad(b_g_up[l].astype(F32), ((0, 2 * LANES - G_LORA), (0, 0)))
    o_a, o_b = _mixers(
        z3, a_conv_w[l].astype(F32), lane_pad(jnp.exp(a_log_rate[l].astype(F32))), lane_pad(a_dt_bias[l]),
        row(a_out_norm[l]), mu_b, mu_s, row(b_w0[l]), wup, row(b_a0[l]), aup, gup,
        row(b_k_k[l]), row(b_k_a[l]), row(b_r_k[l]), row(b_ln_gain[l]), row(b_ln_bias[l]))

    return _out_ffn2(h1, z3, o_a, o_b, w_out[l].astype(BF16),
                     row(ffn2_norm[l]), ffn2_w_gu[l].astype(BF16), ffn2_w_down[l].astype(BF16),
                     row(final_norm), seq=seq, tm=512)
```

```python
import numpy as np

import jax
import jax.numpy as jnp
from jax import lax
from jax.experimental import pallas as pl
from jax.experimental.pallas import tpu as pltpu

F32 = jnp.float32
BF16 = jnp.bfloat16

D_MODEL = 1024
N_META = 16
EPS = 1e-6
D_FF = 2816
CHUNK = 64
A_DK = 128
A_HEADS = 8
A_CONV = 4
B_N = 64
B_HEADS = 16
W_LORA = 64
AA_LORA = 64
G_LORA = 160
B_GN_EPS = B_N * 1e-5

PAD_ROWS = CHUNK - N_META
FF_CHUNK = 256
TAIL = 8
PAIR = 2 * CHUNK
LANES = 128

COL_A = 0
COL_G = 4096
COL_B = 6144
COL_S = 9216
S_BA = 0
S_WA = 128
S_GD = 256
S_COLS = 512
IN_COLS = 9728

VMEM_LIMIT = 56 * 1024 * 1024

NN = (((1,), (0,)), ((), ()))
NT = (((1,), (1,)), ((), ()))
TN = (((0,), (0,)), ((), ()))

M_EYE, M_BLK8, M_OFF16, M_OFF32, M_OFF64, M_STRICT, M_NSTRICT, M_INCL = range(8)


def _mm(a, b, dims=NN):
    return lax.dot_general(a.astype(BF16), b.astype(BF16), dims, preferred_element_type=F32)


def _cumsum_rows(tri, x):
    hi = x.astype(BF16)
    lo = (x - hi.astype(F32)).astype(BF16)
    return jnp.dot(tri, hi, preferred_element_type=F32) + jnp.dot(tri, lo, preferred_element_type=F32)


def _sigmoid(x):
    return 1.0 / (1.0 + jnp.exp(-x))


def _silu(x):
    return x * _sigmoid(x)


def _softplus(x):
    return jnp.maximum(x, 0.0) + jnp.log(1.0 + jnp.exp(-jnp.abs(x)))


def _rms(x, gain):
    return x * lax.rsqrt(jnp.mean(x * x, axis=-1, keepdims=True) + EPS) * gain


def _pair_constants():
    row = np.arange(CHUNK)[:, None]
    col = np.arange(PAIR)[None, :] % CHUNK
    blk = lambda s: (row >> s) == (col >> s)
    strict = row > col
    planes = [row == col, blk(3), blk(4) & ~blk(3), blk(5) & ~blk(4), ~blk(5),
              strict, -1.0 * strict, row >= col]
    lanes = np.arange(LANES)[None, :] < B_N
    halves = [np.broadcast_to(lanes, (CHUNK, LANES)), np.broadcast_to(~lanes, (CHUNK, LANES))]
    return (jnp.asarray(np.stack([np.asarray(p, np.float32) for p in planes]), BF16),
            jnp.asarray(np.stack([np.asarray(h, np.float32) for h in halves]), BF16))


def _tri_inv(ms, mask, half):
    mm = lambda xs, ys: [_mm(x, _stack(y, half)).astype(BF16) for x, y in zip(xs, ys)]

    n1 = [m * mask(M_BLK8) for m in ms]
    n2 = mm(n1, n1)
    n4 = mm(n2, n2)
    d = mm([mask(M_EYE) - a for a in n1], [mask(M_EYE) + a for a in n2])
    d = mm(d, [mask(M_EYE) + a for a in n4])
    for level in (M_OFF16, M_OFF32, M_OFF64):
        de = mm(d, [m * mask(level) for m in ms])
        d = [a - b for a, b in zip(d, mm(de, d))]
    return d


def _stack(x, half):
    xb = x.astype(BF16)
    return jnp.concatenate([xb * half(0), xb * half(1)], axis=0)


def _head_sum(x, left):
    s0 = jnp.sum(jnp.where(left, x, 0.0), axis=-1, keepdims=True)
    s1 = jnp.sum(jnp.where(left, 0.0, x), axis=-1, keepdims=True)
    return jnp.where(left, s0, s1)


def _swiglu(xn, wgu_ref, wd_ref, act_ref):
    for c in range(D_FF // FF_CHUNK):
        lo = c * FF_CHUNK
        g = jnp.dot(xn, wgu_ref[:, lo:lo + FF_CHUNK], preferred_element_type=F32)
        u = jnp.dot(xn, wgu_ref[:, D_FF + lo:D_FF + lo + FF_CHUNK], preferred_element_type=F32)
        act_ref[:, lo:lo + FF_CHUNK] = (_silu(g) * u).astype(BF16)
    return jnp.dot(act_ref[...], wd_ref[...], preferred_element_type=F32)


def _ffn1_rows(h, gain_ref, wgu_ref, wd_ref, mixgain_ref, act_ref):
    xn = _rms(h, gain_ref[...]).astype(BF16)
    h = h + 0.5 * _swiglu(xn, wgu_ref, wd_ref, act_ref)
    return h, _rms(h, mixgain_ref[...]).astype(BF16)


def _ffn1_kernel(x_ref, t_ref, gain_ref, wgu_ref, wd_ref, mixgain_ref, o_ref, u_ref, act_ref):
    last = pl.program_id(1) == pl.num_programs(1) - 1
    rows = lambda h: _ffn1_rows(h, gain_ref, wgu_ref, wd_ref, mixgain_ref, act_ref)

    @pl.when(jnp.logical_not(last))
    def _():
        o_ref[...], u_ref[...] = rows(x_ref[...])

    @pl.when(last)
    def _():
        seq_rows = x_ref.shape[0] - CHUNK
        o_ref[...], u_ref[...] = rows(jnp.concatenate([x_ref[0:seq_rows, :], t_ref[...]], axis=0))


def _ffn1(x, tail, gain, wgu, wd, mixgain):
    bsz, seq, _ = x.shape
    tp = seq + CHUNK
    tm = tp // 4
    const = dict(pipeline_mode=pl.Buffered(1))
    tile = pl.BlockSpec((None, tm, D_MODEL), lambda b, i: (b, i, 0))
    fixed = lambda r, w, **kw: pl.BlockSpec((r, w), lambda b, i: (0, 0), **kw)
    return pl.pallas_call(
        _ffn1_kernel,
        grid=(bsz, tp // tm),
        in_specs=[tile, fixed(CHUNK, D_MODEL), fixed(1, D_MODEL), fixed(D_MODEL, 2 * D_FF, **const),
                  fixed(D_FF, D_MODEL, **const), fixed(1, D_MODEL)],
        out_specs=[tile, tile],
        out_shape=[jax.ShapeDtypeStruct((bsz, tp, D_MODEL), F32),
                   jax.ShapeDtypeStruct((bsz, tp, D_MODEL), BF16)],
        scratch_shapes=[pltpu.VMEM((tm, D_FF), BF16)],
        compiler_params=pltpu.CompilerParams(
            dimension_semantics=("parallel", "parallel"), vmem_limit_bytes=VMEM_LIMIT),
        name="ffn1",
    )(x, tail, gain, wgu, wd, mixgain)


def _inproj_kernel(u_ref, w_ref, o_ref):
    o_ref[...] = jnp.dot(u_ref[...], w_ref[...], preferred_element_type=F32)


def _inproj(u, w, tm, tn):
    n = u.shape[0]
    return pl.pallas_call(
        _inproj_kernel,
        grid=(IN_COLS // tn, n // tm),
        in_specs=[
            pl.BlockSpec((tm, D_MODEL), lambda j, i: (i, 0)),
            pl.BlockSpec((D_MODEL, tn), lambda j, i: (0, j)),
        ],
        out_specs=pl.BlockSpec((tm, tn), lambda j, i: (i, j)),
        out_shape=jax.ShapeDtypeStruct((n, IN_COLS), F32),
        compiler_params=pltpu.CompilerParams(
            dimension_semantics=("parallel", "parallel"), vmem_limit_bytes=VMEM_LIMIT),
        name="in_proj",
    )(u, w)


ROWS = 2


def _mixer_kernel(za_ref, zb_ref, sm_ref, mask_ref, half_ref,
                  convw_ref, rate_ref, dtb_ref, again_ref,
                  mub_ref, mus_ref, w0_ref, wup_ref, a0_ref, aup_ref, gup_ref,
                  kk_ref, ka_ref, rk_ref, lng_ref, lnb_ref,
                  oa_ref, ob_ref,
                  exta_ref, extb_ref, exts_ref, sa_ref, sb_ref):
    c = pl.program_id(1)
    wqkv = 3 * D_MODEL
    ws = S_COLS - S_WA

    @pl.when(c == 0)
    def _():
        sa_ref[...] = jnp.zeros_like(sa_ref)
        sb_ref[...] = jnp.zeros_like(sb_ref)
        exta_ref[...] = jnp.zeros_like(exta_ref)
        extb_ref[...] = jnp.zeros_like(extb_ref)
        exts_ref[...] = jnp.zeros_like(exts_ref)

    mask = lambda i: mask_ref[i]
    half = lambda i: half_ref[i]
    left = lax.broadcasted_iota(jnp.int32, (CHUNK, LANES), 1) < B_N
    tri = lambda: mask_ref[M_INCL, 0:CHUNK, 0:CHUNK]
    incl2 = lambda: jnp.concatenate([mask(M_INCL), mask(M_INCL)], axis=1)
    apairs = range(A_HEADS // 2)
    bpairs = range(B_HEADS // 2)
    blanes = [slice(p * LANES, (p + 1) * LANES) for p in bpairs]
    prow = lax.broadcasted_iota(jnp.int32, (PAIR, PAIR), 0)
    pcol = lax.broadcasted_iota(jnp.int32, (PAIR, PAIR), 1)
    same = (prow >> 6) == (pcol >> 6)
    incl = (lax.broadcasted_iota(jnp.int32, (CHUNK, PAIR), 0)
            >= (lax.broadcasted_iota(jnp.int32, (CHUNK, PAIR), 1) & (CHUNK - 1)))
    lcat = lambda a, b: jnp.concatenate([a, b], axis=1)
    rcat = lambda a, b: jnp.concatenate([a, b], axis=0)

    def diag2(xa, xb):
        za, zb = jnp.zeros(xa.shape, BF16), jnp.zeros(xb.shape, BF16)
        return jnp.concatenate([jnp.concatenate([xa.astype(BF16), zb], axis=1),
                                jnp.concatenate([za, xb.astype(BF16)], axis=1)], axis=0)


    def prepare(bi, tie):
        hold = (lambda n: 0.0) if tie is None else (lambda n: jnp.tile(tie, (1, n // LANES)))
        o = {}
        def shifted(cur, ext_ref, back):
            ext = jnp.concatenate([ext_ref[bi], cur], axis=0)
            return pltpu.roll(ext, back, axis=0)[TAIL:, :]

        curb = zb_ref[bi]
        prevb = shifted(curb, extb_ref, 1)
        extb_ref[bi] = curb[CHUNK - TAIL:, :]
        zb = curb + (prevb - curb) * (mub_ref[...] + hold(wqkv))
        curs = sm_ref[bi, :, S_WA:S_COLS]
        prevs = shifted(curs, exts_ref, 1)
        exts_ref[bi] = curs[CHUNK - TAIL:, :]
        zs = curs + (prevs - curs) * (mus_ref[...] + hold(ws))

        r_all = zb[:, 0:D_MODEL]
        k_all = zb[:, D_MODEL:2 * D_MODEL]
        v_all = zb[:, 2 * D_MODEL:3 * D_MODEL]
        wa = zs[:, 0:LANES]
        gd = zs[:, LANES:3 * LANES]

        logw_all = -float(np.exp(-0.5)) * _sigmoid(w0_ref[...] + _mm(jnp.tanh(wa), wup_ref[...]))
        a_all = _sigmoid(a0_ref[...] + _mm(wa, aup_ref[...]))
        o["gate"] = _mm(_sigmoid(gd), gup_ref[...])
        kk_all = k_all * kk_ref[...]
        k2_all = k_all * (1.0 + (a_all - 1.0) * ka_ref[...])
        cum_all = _cumsum_rows(tri(), logw_all)

        cura = za_ref[bi, :, 0:wqkv]
        conv = (convw_ref[A_CONV - 1:A_CONV, :] + hold(wqkv)) * cura
        for j in range(A_CONV - 1):
            conv = conv + convw_ref[j:j + 1, :] * shifted(cura, exta_ref, A_CONV - 1 - j)
        exta_ref[bi] = cura[CHUNK - TAIL:, :]
        qkv = _silu(conv)

        sm = sm_ref[bi, :, S_BA:S_BA + LANES]
        rows = lax.broadcasted_iota(jnp.int32, (CHUNK, LANES), 0)
        real = jnp.logical_or(c != 0, rows >= PAD_ROWS)
        beta_all = jnp.where(real, _sigmoid(sm), 0.0)
        g_all = jnp.where(real, -rate_ref[...] * _softplus(sm + (dtb_ref[...] + hold(LANES))), 0.0)
        gcum = _cumsum_rows(tri(), g_all)
        gcum_t = gcum.T

        for name in ("r", "k2", "v", "bv", "c_last", "p_tail", "atrt", "v_st", "bk_st"):
            o[name] = []
        for p, sl in enumerate(blanes):
            kk = kk_all[:, sl]
            kk = kk * lax.rsqrt(_head_sum(kk * kk, left) + 1e-6)
            cum = cum_all[:, sl]
            o["r"].append(r_all[:, sl])
            o["k2"].append(k2_all[:, sl])
            o["v"].append(v_all[:, sl])
            o["bv"].append(kk * a_all[:, sl])
            o["c_last"].append(cum[CHUNK - 1:CHUNK, :])
            p_inv = jnp.exp(-cum)
            o["p_tail"].append(jnp.exp(o["c_last"][p] - cum))
            at = (-kk * jnp.exp(cum - logw_all[:, sl])).astype(BF16)
            rt = (o["r"][p] * jnp.exp(cum)).astype(BF16)
            o["atrt"].append(jnp.concatenate([at, rt], axis=0))
            o["v_st"].append(_stack(o["v"][p], half))
            o["bk_st"].append(jnp.concatenate([_stack(o["bv"][p] * p_inv, half),
                                               _stack(o["k2"][p] * p_inv, half)], axis=0))

        for name in ("g_last", "decay", "vbe", "qe", "al", "ar", "ktail"):
            o[name] = []
        for p in apairs:
            qs, ks, kbs, gcs = [], [], [], []
            for h in (2 * p, 2 * p + 1):
                lo = h * A_DK
                q = qkv[:, lo:lo + A_DK]
                k = qkv[:, D_MODEL + lo:D_MODEL + lo + A_DK]
                beta = beta_all[:, h:h + 1]
                gc_h = gcum[:, A_HEADS + h:A_HEADS + h + 1]
                eg = jnp.exp(gc_h)
                gcs.append(gc_h)
                qs.append(q * lax.rsqrt(jnp.sum(q * q, axis=-1, keepdims=True) + 1e-6) * (A_DK ** -0.5))
                ks.append(k * lax.rsqrt(jnp.sum(k * k, axis=-1, keepdims=True) + 1e-6))
                kbs.append(ks[-1] * beta)
                o["vbe"].append(lcat(qkv[:, 2 * D_MODEL + lo:2 * D_MODEL + lo + A_DK] * beta, kbs[-1] * eg))
                o["qe"].append(qs[-1] * eg)
                o["g_last"].append(gc_h[CHUNK - 1:CHUNK, :])
                o["ktail"].append(ks[-1] * jnp.exp(o["g_last"][-1] - gc_h))
            la, lb = A_HEADS + 2 * p, A_HEADS + 2 * p + 1
            gr_p = lcat(gcum_t[la:la + 1, :], gcum_t[lb:lb + 1, :])
            gc_p = jnp.where(left, gcs[0], gcs[1])
            o["decay"].append(jnp.where(incl, jnp.exp(jnp.where(incl, gc_p - gr_p, 0.0)), 0.0))
            o["al"].append(rcat(lcat(kbs[0], kbs[1]), lcat(qs[0], qs[1])))
            o["ar"].append(diag2(ks[0], ks[1]))
        return o

    def grams(bi, o):
        o["bgram"] = [_mm(o["atrt"][p], o["bk_st"][p], NT).astype(BF16) for p in bpairs]
        o["agram"] = [_mm(o["al"][p], o["ar"][p], NT) for p in apairs]
        o["x0"] = [_mm(o["atrt"][p], sb_ref[bi, p], NT) for p in bpairs]
        o["akv"] = [_mm(o["bgram"][p][:CHUNK, PAIR:] * mask(M_STRICT), o["v_st"][p]) for p in bpairs]

    def recur(bi, o):
        bgram, agram, x0 = o["bgram"], o["agram"], o["x0"]
        attn = [agram[p][CHUNK:] * o["decay"][p] for p in apairs]
        tinv = _tri_inv([(agram[p][:CHUNK] * o["decay"][p]).astype(BF16) * mask(M_STRICT) for p in apairs]
                        + [bgram[p][:CHUNK, :PAIR] * mask(M_NSTRICT) for p in bpairs], mask, half)
        atinv, btinv = tinv[:len(apairs)], tinv[len(apairs):]
        uw = [_mm(atinv[p], diag2(o["vbe"][2 * p], o["vbe"][2 * p + 1])) for p in apairs]
        u = [_mm(btinv[p], _stack(x0[p][:CHUNK] + o["akv"][p], half)) for p in bpairs]
        ws_qs = [_mm(rcat(uw[h // 2][:, (2 * (h % 2) + 1) * A_DK:(2 * (h % 2) + 2) * A_DK], o["qe"][h]),
                     sa_ref[bi, h]) for h in range(A_HEADS)]
        o["yy"] = [_mm(bgram[p][CHUNK:, :] * incl2(),
                       jnp.concatenate([_stack(u[p], half), o["v_st"][p]], axis=0)) for p in bpairs]
        o["bupd"] = [_mm(jnp.concatenate([u[p], o["v"][p]], axis=0),
                         jnp.concatenate([o["bv"][p] * o["p_tail"][p], o["k2"][p] * o["p_tail"][p]], axis=0), TN)
                     for p in bpairs]
        v_new = [uw[h // 2][:, 2 * (h % 2) * A_DK:(2 * (h % 2) + 1) * A_DK] - ws_qs[h][:CHUNK]
                 for h in range(A_HEADS)]
        o["ws_qs"] = ws_qs
        o["av"] = [_mm(attn[p], diag2(v_new[2 * p], v_new[2 * p + 1])) for p in apairs]
        o["aupd"] = [_mm(o["ktail"][h], v_new[h], TN) for h in range(A_HEADS)]

    def finish(bi, o):
        for p, sl in enumerate(blanes):
            sb_ref[bi, p] = sb_ref[bi, p] * jnp.exp(o["c_last"][p]) + jnp.where(same, o["bupd"][p], 0.0)
            y = o["x0"][p][CHUNK:] + o["yy"][p]
            mean = _head_sum(y, left) * (1.0 / B_N)
            yc = y - mean
            var = _head_sum(yc * yc, left) * (1.0 / B_N)
            yn = yc * lax.rsqrt(var + B_GN_EPS) * lng_ref[:, sl] + lnb_ref[:, sl]
            bonus = _head_sum(o["r"][p] * o["k2"][p] * rk_ref[:, sl], left) * o["v"][p]
            ob_ref[bi, :, sl] = (yn + bonus) * o["gate"][:, sl]
        for h in range(A_HEADS):
            lo = h * A_DK
            sa_ref[bi, h] = sa_ref[bi, h] * jnp.exp(o["g_last"][h]) + o["aupd"][h]
            out = o["ws_qs"][h][CHUNK:] + o["av"][h // 2][:, (h % 2) * A_DK:(h % 2 + 1) * A_DK]
            out = out * lax.rsqrt(jnp.mean(out * out, axis=-1, keepdims=True) + EPS) * again_ref[...]
            zg = za_ref[bi, :, 3 * D_MODEL + lo:3 * D_MODEL + lo + A_DK]
            oa_ref[bi, :, lo:lo + A_DK] = out * _silu(zg)

    ops = [prepare(0, None)]
    grams(0, ops[0])
    for bi in range(1, ROWS):
        anchor = ops[bi - 1]["bgram"][0][0:16, 0:LANES].astype(F32)[0:1]
        ops.append(prepare(bi, jnp.minimum(jnp.abs(anchor), 0.0)))
        recur(bi - 1, ops[bi - 1])
        grams(bi, ops[bi])
        finish(bi - 1, ops[bi - 1])
    recur(ROWS - 1, ops[ROWS - 1])
    finish(ROWS - 1, ops[ROWS - 1])


def _mixers(z3, convw, rate, dtb, again, mub, mus, w0, wup, a0, aup, gup, kk, ka, rk, lng, lnb):
    bsz, tp, _ = z3.shape
    nc = tp // CHUNK
    chunk = lambda w, j: pl.BlockSpec((ROWS, CHUNK, w), lambda b, c: (b, (c + nc - 1) % nc, j))
    const = lambda r, w: pl.BlockSpec((r, w), lambda b, c: (0, 0))
    out = jax.ShapeDtypeStruct((bsz, tp, D_MODEL), F32)
    masks, halves = _pair_constants()
    return pl.pallas_call(
        _mixer_kernel,
        grid=(bsz // ROWS, nc),
        in_specs=[
            chunk(4 * D_MODEL, COL_A // (4 * D_MODEL)),
            chunk(3 * D_MODEL, COL_B // (3 * D_MODEL)),
            chunk(S_COLS, COL_S // S_COLS),
            pl.BlockSpec(masks.shape, lambda b, c: (0, 0, 0)),
            pl.BlockSpec(halves.shape, lambda b, c: (0, 0, 0)),
            const(A_CONV, 3 * D_MODEL), const(1, LANES), const(1, LANES), const(1, A_DK),
            const(1, 3 * D_MODEL), const(1, S_COLS - S_WA),
            const(1, D_MODEL), const(LANES, D_MODEL), const(1, D_MODEL), const(LANES, D_MODEL),
            const(2 * LANES, D_MODEL),
            const(1, D_MODEL), const(1, D_MODEL), const(1, D_MODEL), const(1, D_MODEL), const(1, D_MODEL),
        ],
        out_specs=[chunk(D_MODEL, 0), chunk(D_MODEL, 0)],
        out_shape=[out, out],
        scratch_shapes=[
            pltpu.VMEM((ROWS, TAIL, 3 * D_MODEL), F32),
            pltpu.VMEM((ROWS, TAIL, 3 * D_MODEL), F32),
            pltpu.VMEM((ROWS, TAIL, S_COLS - S_WA), F32),
            pltpu.VMEM((ROWS, A_HEADS, A_DK, A_DK), F32),
            pltpu.VMEM((ROWS, B_HEADS // 2, PAIR, PAIR), F32),
        ],
        compiler_params=pltpu.CompilerParams(
            dimension_semantics=("parallel", "arbitrary"), vmem_limit_bytes=VMEM_LIMIT),
        name="mixers",
    )(z3, z3, z3, masks, halves, convw, rate, dtb, again, mub, mus, w0, wup, a0, aup, gup, kk, ka, rk, lng, lnb)


def _out_kernel(h_ref, g_ref, oa_ref, ob_ref, wout_ref, gain2_ref, wgu_ref, wd_ref, fgain_ref,
                o_ref, act_ref):
    merged = (_sigmoid(g_ref[:, 0:D_MODEL]) * oa_ref[...]
              + _sigmoid(g_ref[:, D_MODEL:2 * D_MODEL]) * ob_ref[...])
    h = h_ref[...] + jnp.dot(merged.astype(BF16), wout_ref[...], preferred_element_type=F32)
    xn = _rms(h, gain2_ref[...]).astype(BF16)
    h = h + 0.5 * _swiglu(xn, wgu_ref, wd_ref, act_ref)
    o_ref[...] = _rms(h, fgain_ref[...])


def _out_ffn2(h3, z3, oa, ob, wout, gain2, wgu, wd, fgain, seq, tm):
    bsz = h3.shape[0]
    const = dict(pipeline_mode=pl.Buffered(1))
    tile = lambda w, j: pl.BlockSpec((None, tm, w), lambda b, i: (b, i, j))
    return pl.pallas_call(
        _out_kernel,
        grid=(bsz, seq // tm),
        in_specs=[
            tile(D_MODEL, 0),
            tile(2 * D_MODEL, COL_G // (2 * D_MODEL)),
            tile(D_MODEL, 0),
            tile(D_MODEL, 0),
            pl.BlockSpec((D_MODEL, D_MODEL), lambda b, i: (0, 0), **const),
            pl.BlockSpec((1, D_MODEL), lambda b, i: (0, 0)),
            pl.BlockSpec((D_MODEL, 2 * D_FF), lambda b, i: (0, 0), **const),
            pl.BlockSpec((D_FF, D_MODEL), lambda b, i: (0, 0), **const),
            pl.BlockSpec((1, D_MODEL), lambda b, i: (0, 0)),
        ],
        out_specs=tile(D_MODEL, 0),
        out_shape=jax.ShapeDtypeStruct((bsz, seq, D_MODEL), F32),
        scratch_shapes=[pltpu.VMEM((tm, D_FF), BF16)],
        compiler_params=pltpu.CompilerParams(
            dimension_semantics=("parallel", "parallel"), vmem_limit_bytes=VMEM_LIMIT),
        name="out_ffn2",
    )(h3, z3, oa, ob, wout, gain2, wgu, wd, fgain)


def _regroup_in_weight(w):
    b0 = 4 * D_MODEL + 2 * A_HEADS
    lora0 = b0 + 3 * D_MODEL
    gate0 = lora0 + W_LORA + AA_LORA + G_LORA
    zeros = lambda n: jnp.zeros((w.shape[0], n), w.dtype)
    return jnp.concatenate([
        w[:, 0:4 * D_MODEL],
        w[:, gate0:gate0 + 2 * D_MODEL],
        w[:, b0:lora0],
        w[:, 4 * D_MODEL:b0], zeros(LANES - 2 * A_HEADS),
        w[:, lora0:lora0 + W_LORA + AA_LORA],
        w[:, lora0 + W_LORA + AA_LORA:gate0], zeros(2 * LANES - G_LORA),
    ], axis=1)


def kernel(x, meta_tokens, ffn1_norm, ffn1_w_gu, ffn1_w_down, mix_norm, w_in, a_conv_w, a_log_rate, a_dt_bias, a_out_norm, b_shift_mu, b_w0, b_w_up, b_a0, b_a_up, b_g_up, b_k_k, b_k_a, b_r_k, b_ln_gain, b_ln_bias, w_out, ffn2_norm, ffn2_w_gu, ffn2_w_down, final_norm):
    bsz, seq, _ = x.shape
    tp = seq + CHUNK
    row = lambda v: v.reshape(1, -1).astype(F32)

    tail = jnp.concatenate([jnp.zeros((PAD_ROWS, D_MODEL), x.dtype), meta_tokens.astype(x.dtype)], axis=0)

    l = 0
    h1, u = _ffn1(x, tail, row(ffn1_norm[l]), ffn1_w_gu[l].astype(BF16), ffn1_w_down[l].astype(BF16),
                  row(mix_norm[l]))
    z = _inproj(u.reshape(bsz * tp, D_MODEL), _regroup_in_weight(w_in[l]).astype(BF16), tm=1280, tn=2432)
    z3 = z.reshape(bsz, tp, IN_COLS)

    lane_pad = lambda v: jnp.pad(v.astype(F32), (A_HEADS, LANES - 2 * A_HEADS)).reshape(1, LANES)
    mu = b_shift_mu[l].astype(F32)
    mu_b = mu[:3 * D_MODEL].reshape(1, -1)
    mu_s = jnp.pad(mu[3 * D_MODEL:], (0, 2 * LANES - G_LORA)).reshape(1, -1)
    wup = jnp.pad(b_w_up[l].astype(F32), ((0, AA_LORA), (0, 0)))
    aup = jnp.pad(b_a_up[l].astype(F32), ((W_LORA, 0), (0, 0)))
    gup = jnp.pad(b_g_up[l].astype(F32), ((0, 2 * LANES - G_LORA), (0, 0)))
    o_a, o_b = _mixers(
        z3, a_conv_w[l].astype(F32), lane_pad(jnp.exp(a_log_rate[l].astype(F32))), lane_pad(a_dt_bias[l]),
        row(a_out_norm[l]), mu_b, mu_s, row(b_w0[l]), wup, row(b_a0[l]), aup, gup,
        row(b_k_k[l]), row(b_k_a[l]), row(b_r_k[l]), row(b_ln_gain[l]), row(b_ln_bias[l]))

    return _out_ffn2(h1, z3, o_a, o_b, w_out[l].astype(BF16),
                     row(ffn2_norm[l]), ffn2_w_gu[l].astype(BF16), ffn2_w_down[l].astype(BF16),
                     row(final_norm), seq=seq, tm=512)
```

```python
import numpy as np

import jax
import jax.numpy as jnp
from jax import lax
from jax.experimental import pallas as pl
from jax.experimental.pallas import tpu as pltpu

F32 = jnp.float32
BF16 = jnp.bfloat16

D_MODEL = 1024
N_META = 16
EPS = 1e-6
D_FF = 2816
CHUNK = 64
A_DK = 128
A_HEADS = 8
A_CONV = 4
B_N = 64
B_HEADS = 16
W_LORA = 64
AA_LORA = 64
G_LORA = 160
B_GN_EPS = B_N * 1e-5

PAD_ROWS = CHUNK - N_META
FF_CHUNK = 256
TAIL = 8
PAIR = 2 * CHUNK
LANES = 128

COL_A = 0
COL_G = 4096
COL_B = 6144
COL_S = 9216
S_BA = 0
S_WA = 128
S_GD = 256
S_COLS = 512
IN_COLS = 9728

VMEM_LIMIT = 56 * 1024 * 1024

NN = (((1,), (0,)), ((), ()))
NT = (((1,), (1,)), ((), ()))
TN = (((0,), (0,)), ((), ()))

M_EYE, M_BLK8, M_OFF16, M_OFF32, M_OFF64, M_STRICT, M_NSTRICT, M_INCL = range(8)


def _mm(a, b, dims=NN):
    return lax.dot_general(a.astype(BF16), b.astype(BF16), dims, preferred_element_type=F32)


def _cumsum_rows(tri, x):
    hi = x.astype(BF16)
    lo = (x - hi.astype(F32)).astype(BF16)
    return jnp.dot(tri, hi, preferred_element_type=F32) + jnp.dot(tri, lo, preferred_element_type=F32)


def _sigmoid(x):
    return 1.0 / (1.0 + jnp.exp(-x))


def _silu(x):
    return x * _sigmoid(x)


def _softplus(x):
    return jnp.maximum(x, 0.0) + jnp.log(1.0 + jnp.exp(-jnp.abs(x)))


def _rms(x, gain):
    return x * lax.rsqrt(jnp.mean(x * x, axis=-1, keepdims=True) + EPS) * gain


def _pair_constants():
    row = np.arange(CHUNK)[:, None]
    col = np.arange(PAIR)[None, :] % CHUNK
    blk = lambda s: (row >> s) == (col >> s)
    strict = row > col
    planes = [row == col, blk(3), blk(4) & ~blk(3), blk(5) & ~blk(4), ~blk(5),
              strict, -1.0 * strict, row >= col]
    lanes = np.arange(LANES)[None, :] < B_N
    halves = [np.broadcast_to(lanes, (CHUNK, LANES)), np.broadcast_to(~lanes, (CHUNK, LANES))]
    return (jnp.asarray(np.stack([np.asarray(p, np.float32) for p in planes]), BF16),
            jnp.asarray(np.stack([np.asarray(h, np.float32) for h in halves]), BF16))


def _tri_inv(ms, mask, half):
    mm = lambda xs, ys: [_mm(x, _stack(y, half)).astype(BF16) for x, y in zip(xs, ys)]

    n1 = [m * mask(M_BLK8) for m in ms]
    n2 = mm(n1, n1)
    n4 = mm(n2, n2)
    d = mm([mask(M_EYE) - a for a in n1], [mask(M_EYE) + a for a in n2])
    d = mm(d, [mask(M_EYE) + a for a in n4])
    for level in (M_OFF16, M_OFF32, M_OFF64):
        de = mm(d, [m * mask(level) for m in ms])
        d = [a - b for a, b in zip(d, mm(de, d))]
    return d


def _stack(x, half):
    xb = x.astype(BF16)
    return jnp.concatenate([xb * half(0), xb * half(1)], axis=0)


def _head_sum(x, left):
    s0 = jnp.sum(jnp.where(left, x, 0.0), axis=-1, keepdims=True)
    s1 = jnp.sum(jnp.where(left, 0.0, x), axis=-1, keepdims=True)
    return jnp.where(left, s0, s1)


def _swiglu(xn, wgu_ref, wd_ref, act_ref):
    for c in range(D_FF // FF_CHUNK):
        lo = c * FF_CHUNK
        g = jnp.dot(xn, wgu_ref[:, lo:lo + FF_CHUNK], preferred_element_type=F32)
        u = jnp.dot(xn, wgu_ref[:, D_FF + lo:D_FF + lo + FF_CHUNK], preferred_element_type=F32)
        act_ref[:, lo:lo + FF_CHUNK] = (_silu(g) * u).astype(BF16)
    return jnp.dot(act_ref[...], wd_ref[...], preferred_element_type=F32)


def _ffn1_rows(h, gain_ref, wgu_ref, wd_ref, mixgain_ref, act_ref):
    xn = _rms(h, gain_ref[...]).astype(BF16)
    h = h + 0.5 * _swiglu(xn, wgu_ref, wd_ref, act_ref)
    return h, _rms(h, mixgain_ref[...]).astype(BF16)


def _ffn1_kernel(x_ref, t_ref, gain_ref, wgu_ref, wd_ref, mixgain_ref, o_ref, u_ref, act_ref):
    last = pl.program_id(1) == pl.num_programs(1) - 1
    rows = lambda h: _ffn1_rows(h, gain_ref, wgu_ref, wd_ref, mixgain_ref, act_ref)

    @pl.when(jnp.logical_not(last))
    def _():
        o_ref[...], u_ref[...] = rows(x_ref[...])

    @pl.when(last)
    def _():
        seq_rows = x_ref.shape[0] - CHUNK
        o_ref[...], u_ref[...] = rows(jnp.concatenate([x_ref[0:seq_rows, :], t_ref[...]], axis=0))


def _ffn1(x, tail, gain, wgu, wd, mixgain):
    bsz, seq, _ = x.shape
    tp = seq + CHUNK
    tm = tp // 4
    const = dict(pipeline_mode=pl.Buffered(1))
    tile = pl.BlockSpec((None, tm, D_MODEL), lambda b, i: (b, i, 0))
    fixed = lambda r, w, **kw: pl.BlockSpec((r, w), lambda b, i: (0, 0), **kw)
    return pl.pallas_call(
        _ffn1_kernel,
        grid=(bsz, tp // tm),
        in_specs=[tile, fixed(CHUNK, D_MODEL), fixed(1, D_MODEL), fixed(D_MODEL, 2 * D_FF, **const),
                  fixed(D_FF, D_MODEL, **const), fixed(1, D_MODEL)],
        out_specs=[tile, tile],
        out_shape=[jax.ShapeDtypeStruct((bsz, tp, D_MODEL), F32),
                   jax.ShapeDtypeStruct((bsz, tp, D_MODEL), BF16)],
        scratch_shapes=[pltpu.VMEM((tm, D_FF), BF16)],
        compiler_params=pltpu.CompilerParams(
            dimension_semantics=("parallel", "parallel"), vmem_limit_bytes=VMEM_LIMIT),
        name="ffn1",
    )(x, tail, gain, wgu, wd, mixgain)


def _inproj_kernel(u_ref, w_ref, o_ref):
    o_ref[...] = jnp.dot(u_ref[...], w_ref[...], preferred_element_type=F32)


def _inproj(u, w, tm, tn):
    n = u.shape[0]
    return pl.pallas_call(
        _inproj_kernel,
        grid=(IN_COLS // tn, n // tm),
        in_specs=[
            pl.BlockSpec((tm, D_MODEL), lambda j, i: (i, 0)),
            pl.BlockSpec((D_MODEL, tn), lambda j, i: (0, j)),
        ],
        out_specs=pl.BlockSpec((tm, tn), lambda j, i: (i, j)),
        out_shape=jax.ShapeDtypeStruct((n, IN_COLS), F32),
        compiler_params=pltpu.CompilerParams(
            dimension_semantics=("parallel", "parallel"), vmem_limit_bytes=VMEM_LIMIT),
        name="in_proj",
    )(u, w)


ROWS = 4


def _mixer_kernel(za_ref, zb_ref, sm_ref, mask_ref, half_ref,
                  convw_ref, rate_ref, dtb_ref, again_ref,
                  mub_ref, mus_ref, w0_ref, wup_ref, a0_ref, aup_ref, gup_ref,
                  kk_ref, ka_ref, rk_ref, lng_ref, lnb_ref,
                  oa_ref, ob_ref,
                  exta_ref, extb_ref, exts_ref, sa_ref, sb_ref):
    c = pl.program_id(1)
    wqkv = 3 * D_MODEL
    ws = S_COLS - S_WA

    @pl.when(c == 0)
    def _():
        sa_ref[...] = jnp.zeros_like(sa_ref)
        sb_ref[...] = jnp.zeros_like(sb_ref)
        exta_ref[...] = jnp.zeros_like(exta_ref)
        extb_ref[...] = jnp.zeros_like(extb_ref)
        exts_ref[...] = jnp.zeros_like(exts_ref)

    mask = lambda i: mask_ref[i]
    half = lambda i: half_ref[i]
    left = lax.broadcasted_iota(jnp.int32, (CHUNK, LANES), 1) < B_N
    tri = lambda: mask_ref[M_INCL, 0:CHUNK, 0:CHUNK]
    incl2 = lambda: jnp.concatenate([mask(M_INCL), mask(M_INCL)], axis=1)
    apairs = range(A_HEADS // 2)
    bpairs = range(B_HEADS // 2)
    blanes = [slice(p * LANES, (p + 1) * LANES) for p in bpairs]
    prow = lax.broadcasted_iota(jnp.int32, (PAIR, PAIR), 0)
    pcol = lax.broadcasted_iota(jnp.int32, (PAIR, PAIR), 1)
    same = (prow >> 6) == (pcol >> 6)
    incl = (lax.broadcasted_iota(jnp.int32, (CHUNK, PAIR), 0)
            >= (lax.broadcasted_iota(jnp.int32, (CHUNK, PAIR), 1) & (CHUNK - 1)))
    lcat = lambda a, b: jnp.concatenate([a, b], axis=1)
    rcat = lambda a, b: jnp.concatenate([a, b], axis=0)

    def diag2(xa, xb):
        za, zb = jnp.zeros(xa.shape, BF16), jnp.zeros(xb.shape, BF16)
        return jnp.concatenate([jnp.concatenate([xa.astype(BF16), zb], axis=1),
                                jnp.concatenate([za, xb.astype(BF16)], axis=1)], axis=0)


    def prepare(bi, tie):
        hold = (lambda n: 0.0) if tie is None else (lambda n: jnp.tile(tie, (1, n // LANES)))
        o = {}
        def shifted(cur, ext_ref, back):
            ext = jnp.concatenate([ext_ref[bi], cur], axis=0)
            return pltpu.roll(ext, back, axis=0)[TAIL:, :]

        curb = zb_ref[bi]
        prevb = shifted(curb, extb_ref, 1)
        extb_ref[bi] = curb[CHUNK - TAIL:, :]
        zb = curb + (prevb - curb) * (mub_ref[...] + hold(wqkv))
        curs = sm_ref[bi, :, S_WA:S_COLS]
        prevs = shifted(curs, exts_ref, 1)
        exts_ref[bi] = curs[CHUNK - TAIL:, :]
        zs = curs + (prevs - curs) * (mus_ref[...] + hold(ws))

        r_all = zb[:, 0:D_MODEL]
        k_all = zb[:, D_MODEL:2 * D_MODEL]
        v_all = zb[:, 2 * D_MODEL:3 * D_MODEL]
        wa = zs[:, 0:LANES]
        gd = zs[:, LANES:3 * LANES]

        logw_all = -float(np.exp(-0.5)) * _sigmoid(w0_ref[...] + _mm(jnp.tanh(wa), wup_ref[...]))
        a_all = _sigmoid(a0_ref[...] + _mm(wa, aup_ref[...]))
        o["gate"] = _mm(_sigmoid(gd), gup_ref[...])
        kk_all = k_all * kk_ref[...]
        k2_all = k_all * (1.0 + (a_all - 1.0) * ka_ref[...])
        cum_all = _cumsum_rows(tri(), logw_all)

        cura = za_ref[bi, :, 0:wqkv]
        conv = (convw_ref[A_CONV - 1:A_CONV, :] + hold(wqkv)) * cura
        for j in range(A_CONV - 1):
            conv = conv + convw_ref[j:j + 1, :] * shifted(cura, exta_ref, A_CONV - 1 - j)
        exta_ref[bi] = cura[CHUNK - TAIL:, :]
        qkv = _silu(conv)

        sm = sm_ref[bi, :, S_BA:S_BA + LANES]
        rows = lax.broadcasted_iota(jnp.int32, (CHUNK, LANES), 0)
        real = jnp.logical_or(c != 0, rows >= PAD_ROWS)
        beta_all = jnp.where(real, _sigmoid(sm), 0.0)
        g_all = jnp.where(real, -rate_ref[...] * _softplus(sm + (dtb_ref[...] + hold(LANES))), 0.0)
        gcum = _cumsum_rows(tri(), g_all)
        gcum_t = gcum.T

        for name in ("r", "k2", "v", "bv", "c_last", "p_tail", "atrt", "v_st", "bk_st"):
            o[name] = []
        for p, sl in enumerate(blanes):
            kk = kk_all[:, sl]
            kk = kk * lax.rsqrt(_head_sum(kk * kk, left) + 1e-6)
            cum = cum_all[:, sl]
            o["r"].append(r_all[:, sl])
            o["k2"].append(k2_all[:, sl])
            o["v"].append(v_all[:, sl])
            o["bv"].append(kk * a_all[:, sl])
            o["c_last"].append(cum[CHUNK - 1:CHUNK, :])
            p_inv = jnp.exp(-cum)
            o["p_tail"].append(jnp.exp(o["c_last"][p] - cum))
            at = (-kk * jnp.exp(cum - logw_all[:, sl])).astype(BF16)
            rt = (o["r"][p] * jnp.exp(cum)).astype(BF16)
            o["atrt"].append(jnp.concatenate([at, rt], axis=0))
            o["v_st"].append(_stack(o["v"][p], half))
            o["bk_st"].append(jnp.concatenate([_stack(o["bv"][p] * p_inv, half),
                                               _stack(o["k2"][p] * p_inv, half)], axis=0))

        for name in ("g_last", "decay", "vbe", "qe", "al", "ar", "ktail"):
            o[name] = []
        for p in apairs:
            qs, ks, kbs, gcs = [], [], [], []
            for h in (2 * p, 2 * p + 1):
                lo = h * A_DK
                q = qkv[:, lo:lo + A_DK]
                k = qkv[:, D_MODEL + lo:D_MODEL + lo + A_DK]
                beta = beta_all[:, h:h + 1]
                gc_h = gcum[:, A_HEADS + h:A_HEADS + h + 1]
                eg = jnp.exp(gc_h)
                gcs.append(gc_h)
                qs.append(q * lax.rsqrt(jnp.sum(q * q, axis=-1, keepdims=True) + 1e-6) * (A_DK ** -0.5))
                ks.append(k * lax.rsqrt(jnp.sum(k * k, axis=-1, keepdims=True) + 1e-6))
                kbs.append(ks[-1] * beta)
                o["vbe"].append(lcat(qkv[:, 2 * D_MODEL + lo:2 * D_MODEL + lo + A_DK] * beta, kbs[-1] * eg))
                o["qe"].append(qs[-1] * eg)
                o["g_last"].append(gc_h[CHUNK - 1:CHUNK, :])
                o["ktail"].append(ks[-1] * jnp.exp(o["g_last"][-1] - gc_h))
            la, lb = A_HEADS + 2 * p, A_HEADS + 2 * p + 1
            gr_p = lcat(gcum_t[la:la + 1, :], gcum_t[lb:lb + 1, :])
            gc_p = jnp.where(left, gcs[0], gcs[1])
            o["decay"].append(jnp.where(incl, jnp.exp(jnp.where(incl, gc_p - gr_p, 0.0)), 0.0))
            o["al"].append(rcat(lcat(kbs[0], kbs[1]), lcat(qs[0], qs[1])))
            o["ar"].append(diag2(ks[0], ks[1]))
        return o

    def grams(bi, o):
        o["bgram"] = [_mm(o["atrt"][p], o["bk_st"][p], NT).astype(BF16) for p in bpairs]
        o["agram"] = [_mm(o["al"][p], o["ar"][p], NT) for p in apairs]
        o["x0"] = [_mm(o["atrt"][p], sb_ref[bi, p], NT) for p in bpairs]
        o["akv"] = [_mm(o["bgram"][p][:CHUNK, PAIR:] * mask(M_STRICT), o["v_st"][p]) for p in bpairs]

    def recur(bi, o):
        bgram, agram, x0 = o["bgram"], o["agram"], o["x0"]
        attn = [agram[p][CHUNK:] * o["decay"][p] for p in apairs]
        tinv = _tri_inv([(agram[p][:CHUNK] * o["decay"][p]).astype(BF16) * mask(M_STRICT) for p in apairs]
                        + [bgram[p][:CHUNK, :PAIR] * mask(M_NSTRICT) for p in bpairs], mask, half)
        atinv, btinv = tinv[:len(apairs)], tinv[len(apairs):]
        uw = [_mm(atinv[p], diag2(o["vbe"][2 * p], o["vbe"][2 * p + 1])) for p in apairs]
        u = [_mm(btinv[p], _stack(x0[p][:CHUNK] + o["akv"][p], half)) for p in bpairs]
        ws_qs = [_mm(rcat(uw[h // 2][:, (2 * (h % 2) + 1) * A_DK:(2 * (h % 2) + 2) * A_DK], o["qe"][h]),
                     sa_ref[bi, h]) for h in range(A_HEADS)]
        o["yy"] = [_mm(bgram[p][CHUNK:, :] * incl2(),
                       jnp.concatenate([_stack(u[p], half), o["v_st"][p]], axis=0)) for p in bpairs]
        o["bupd"] = [_mm(jnp.concatenate([u[p], o["v"][p]], axis=0),
                         jnp.concatenate([o["bv"][p] * o["p_tail"][p], o["k2"][p] * o["p_tail"][p]], axis=0), TN)
                     for p in bpairs]
        v_new = [uw[h // 2][:, 2 * (h % 2) * A_DK:(2 * (h % 2) + 1) * A_DK] - ws_qs[h][:CHUNK]
                 for h in range(A_HEADS)]
        o["ws_qs"] = ws_qs
        o["av"] = [_mm(attn[p], diag2(v_new[2 * p], v_new[2 * p + 1])) for p in apairs]
        o["aupd"] = [_mm(o["ktail"][h], v_new[h], TN) for h in range(A_HEADS)]

    def finish(bi, o):
        for p, sl in enumerate(blanes):
            sb_ref[bi, p] = sb_ref[bi, p] * jnp.exp(o["c_last"][p]) + jnp.where(same, o["bupd"][p], 0.0)
            y = o["x0"][p][CHUNK:] + o["yy"][p]
            mean = _head_sum(y, left) * (1.0 / B_N)
            yc = y - mean
            var = _head_sum(yc * yc, left) * (1.0 / B_N)
            yn = yc * lax.rsqrt(var + B_GN_EPS) * lng_ref[:, sl] + lnb_ref[:, sl]
            bonus = _head_sum(o["r"][p] * o["k2"][p] * rk_ref[:, sl], left) * o["v"][p]
            ob_ref[bi, :, sl] = (yn + bonus) * o["gate"][:, sl]
        for h in range(A_HEADS):
            lo = h * A_DK
            sa_ref[bi, h] = sa_ref[bi, h] * jnp.exp(o["g_last"][h]) + o["aupd"][h]
            out = o["ws_qs"][h][CHUNK:] + o["av"][h // 2][:, (h % 2) * A_DK:(h % 2 + 1) * A_DK]
            out = out * lax.rsqrt(jnp.mean(out * out, axis=-1, keepdims=True) + EPS) * again_ref[...]
            zg = za_ref[bi, :, 3 * D_MODEL + lo:3 * D_MODEL + lo + A_DK]
            oa_ref[bi, :, lo:lo + A_DK] = out * _silu(zg)

    ops = [prepare(0, None)]
    grams(0, ops[0])
    for bi in range(1, ROWS):
        anchor = ops[bi - 1]["bgram"][0][0:16, 0:LANES].astype(F32)[0:1]
        ops.append(prepare(bi, jnp.minimum(jnp.abs(anchor), 0.0)))
        recur(bi - 1, ops[bi - 1])
        grams(bi, ops[bi])
        finish(bi - 1, ops[bi - 1])
    recur(ROWS - 1, ops[ROWS - 1])
    finish(ROWS - 1, ops[ROWS - 1])


def _mixers(z3, convw, rate, dtb, again, mub, mus, w0, wup, a0, aup, gup, kk, ka, rk, lng, lnb):
    bsz, tp, _ = z3.shape
    nc = tp // CHUNK
    chunk = lambda w, j: pl.BlockSpec((ROWS, CHUNK, w), lambda b, c: (b, (c + nc - 1) % nc, j))
    const = lambda r, w: pl.BlockSpec((r, w), lambda b, c: (0, 0))
    out = jax.ShapeDtypeStruct((bsz, tp, D_MODEL), F32)
    masks, halves = _pair_constants()
    return pl.pallas_call(
        _mixer_kernel,
        grid=(bsz // ROWS, nc),
        in_specs=[
            chunk(4 * D_MODEL, COL_A // (4 * D_MODEL)),
            chunk(3 * D_MODEL, COL_B // (3 * D_MODEL)),
            chunk(S_COLS, COL_S // S_COLS),
            pl.BlockSpec(masks.shape, lambda b, c: (0, 0, 0)),
            pl.BlockSpec(halves.shape, lambda b, c: (0, 0, 0)),
            const(A_CONV, 3 * D_MODEL), const(1, LANES), const(1, LANES), const(1, A_DK),
            const(1, 3 * D_MODEL), const(1, S_COLS - S_WA),
            const(1, D_MODEL), const(LANES, D_MODEL), const(1, D_MODEL), const(LANES, D_MODEL),
            const(2 * LANES, D_MODEL),
            const(1, D_MODEL), const(1, D_MODEL), const(1, D_MODEL), const(1, D_MODEL), const(1, D_MODEL),
        ],
        out_specs=[chunk(D_MODEL, 0), chunk(D_MODEL, 0)],
        out_shape=[out, out],
        scratch_shapes=[
            pltpu.VMEM((ROWS, TAIL, 3 * D_MODEL), F32),
            pltpu.VMEM((ROWS, TAIL, 3 * D_MODEL), F32),
            pltpu.VMEM((ROWS, TAIL, S_COLS - S_WA), F32),
            pltpu.VMEM((ROWS, A_HEADS, A_DK, A_DK), F32),
            pltpu.VMEM((ROWS, B_HEADS // 2, PAIR, PAIR), F32),
        ],
        compiler_params=pltpu.CompilerParams(
            dimension_semantics=("parallel", "arbitrary"), vmem_limit_bytes=VMEM_LIMIT),
        name="mixers",
    )(z3, z3, z3, masks, halves, convw, rate, dtb, again, mub, mus, w0, wup, a0, aup, gup, kk, ka, rk, lng, lnb)


def _out_kernel(h_ref, g_ref, oa_ref, ob_ref, wout_ref, gain2_ref, wgu_ref, wd_ref, fgain_ref,
                o_ref, act_ref):
    merged = (_sigmoid(g_ref[:, 0:D_MODEL]) * oa_ref[...]
              + _sigmoid(g_ref[:, D_MODEL:2 * D_MODEL]) * ob_ref[...])
    h = h_ref[...] + jnp.dot(merged.astype(BF16), wout_ref[...], preferred_element_type=F32)
    xn = _rms(h, gain2_ref[...]).astype(BF16)
    h = h + 0.5 * _swiglu(xn, wgu_ref, wd_ref, act_ref)
    o_ref[...] = _rms(h, fgain_ref[...])


def _out_ffn2(h3, z3, oa, ob, wout, gain2, wgu, wd, fgain, seq, tm):
    bsz = h3.shape[0]
    const = dict(pipeline_mode=pl.Buffered(1))
    tile = lambda w, j: pl.BlockSpec((None, tm, w), lambda b, i: (b, i, j))
    return pl.pallas_call(
        _out_kernel,
        grid=(bsz, seq // tm),
        in_specs=[
            tile(D_MODEL, 0),
            tile(2 * D_MODEL, COL_G // (2 * D_MODEL)),
            tile(D_MODEL, 0),
            tile(D_MODEL, 0),
            pl.BlockSpec((D_MODEL, D_MODEL), lambda b, i: (0, 0), **const),
            pl.BlockSpec((1, D_MODEL), lambda b, i: (0, 0)),
            pl.BlockSpec((D_MODEL, 2 * D_FF), lambda b, i: (0, 0), **const),
            pl.BlockSpec((D_FF, D_MODEL), lambda b, i: (0, 0), **const),
            pl.BlockSpec((1, D_MODEL), lambda b, i: (0, 0)),
        ],
        out_specs=tile(D_MODEL, 0),
        out_shape=jax.ShapeDtypeStruct((bsz, seq, D_MODEL), F32),
        scratch_shapes=[pltpu.VMEM((tm, D_FF), BF16)],
        compiler_params=pltpu.CompilerParams(
            dimension_semantics=("parallel", "parallel"), vmem_limit_bytes=VMEM_LIMIT),
        name="out_ffn2",
    )(h3, z3, oa, ob, wout, gain2, wgu, wd, fgain)


def _regroup_in_weight(w):
    b0 = 4 * D_MODEL + 2 * A_HEADS
    lora0 = b0 + 3 * D_MODEL
    gate0 = lora0 + W_LORA + AA_LORA + G_LORA
    zeros = lambda n: jnp.zeros((w.shape[0], n), w.dtype)
    return jnp.concatenate([
        w[:, 0:4 * D_MODEL],
        w[:, gate0:gate0 + 2 * D_MODEL],
        w[:, b0:lora0],
        w[:, 4 * D_MODEL:b0], zeros(LANES - 2 * A_HEADS),
        w[:, lora0:lora0 + W_LORA + AA_LORA],
        w[:, lora0 + W_LORA + AA_LORA:gate0], zeros(2 * LANES - G_LORA),
    ], axis=1)


def kernel(x, meta_tokens, ffn1_norm, ffn1_w_gu, ffn1_w_down, mix_norm, w_in, a_conv_w, a_log_rate, a_dt_bias, a_out_norm, b_shift_mu, b_w0, b_w_up, b_a0, b_a_up, b_g_up, b_k_k, b_k_a, b_r_k, b_ln_gain, b_ln_bias, w_out, ffn2_norm, ffn2_w_gu, ffn2_w_down, final_norm):
    bsz, seq, _ = x.shape
    tp = seq + CHUNK
    row = lambda v: v.reshape(1, -1).astype(F32)

    tail = jnp.concatenate([jnp.zeros((PAD_ROWS, D_MODEL), x.dtype), meta_tokens.astype(x.dtype)], axis=0)

    l = 0
    h1, u = _ffn1(x, tail, row(ffn1_norm[l]), ffn1_w_gu[l].astype(BF16), ffn1_w_down[l].astype(BF16),
                  row(mix_norm[l]))
    z = _inproj(u.reshape(bsz * tp, D_MODEL), _regroup_in_weight(w_in[l]).astype(BF16), tm=1280, tn=2432)
    z3 = z.reshape(bsz, tp, IN_COLS)

    lane_pad = lambda v: jnp.pad(v.astype(F32), (A_HEADS, LANES - 2 * A_HEADS)).reshape(1, LANES)
    mu = b_shift_mu[l].astype(F32)
    mu_b = mu[:3 * D_MODEL].reshape(1, -1)
    mu_s = jnp.pad(mu[3 * D_MODEL:], (0, 2 * LANES - G_LORA)).reshape(1, -1)
    wup = jnp.pad(b_w_up[l].astype(F32), ((0, AA_LORA), (0, 0)))
    aup = jnp.pad(b_a_up[l].astype(F32), ((W_LORA, 0), (0, 0)))
    gup = jnp.pad(b_g_up[l].astype(F32), ((0, 2 * LANES - G_LORA), (0, 0)))
    o_a, o_b = _mixers(
        z3, a_conv_w[l].astype(F32), lane_pad(jnp.exp(a_log_rate[l].astype(F32))), lane_pad(a_dt_bias[l]),
        row(a_out_norm[l]), mu_b, mu_s, row(b_w0[l]), wup, row(b_a0[l]), aup, gup,
        row(b_k_k[l]), row(b_k_a[l]), row(b_r_k[l]), row(b_ln_gain[l]), row(b_ln_bias[l]))

    return _out_ffn2(h1, z3, o_a, o_b, w_out[l].astype(BF16),
                     row(ffn2_norm[l]), ffn2_w_gu[l].astype(BF16), ffn2_w_down[l].astype(BF16),
                     row(final_norm), seq=seq, tm=512)
```
